```python
import jax, jax.numpy as jnp
from jax import lax
import numpy as np

D_MODEL = 2048
BATCH = 4
SEQ = 2048
DEPTH = 4
DEC_BATCH = 128
DEC_SEQ = 1
PAST_LEN = 16384
PAGE_SIZE = 128

MIX_WIDTH = D_MODEL
A_WIDTH = MIX_WIDTH // 2
B_WIDTH = MIX_WIDTH - A_WIDTH
A_HEADS = 8
A_HEAD_DIM = A_WIDTH // A_HEADS
A_CHUNK = 128
B_HEADS = 8
B_KEY_DIM = 128
B_VAL_DIM = B_WIDTH // B_HEADS
B_QF_WIDTH = B_HEADS * B_KEY_DIM
B_CHUNK = 64
D_FF = -(-8 * D_MODEL // (3 * 256)) * 256
IN_WIDTH = 2 * A_WIDTH + 2 * B_QF_WIDTH + 2 * B_WIDTH
SPLITS = [A_WIDTH, 2 * A_WIDTH, 2 * A_WIDTH + B_QF_WIDTH, 2 * A_WIDTH + 2 * B_QF_WIDTH,
          2 * A_WIDTH + 2 * B_QF_WIDTH + B_WIDTH]
EPS = 1e-6

kernel_name = "hybrid_gmlp_hgrn2_decode_step"


def rms_norm(x, g):
    x32 = x.astype(jnp.float32)
    y = x32 * lax.rsqrt(jnp.mean(x32 * x32, axis=-1, keepdims=True) + EPS)
    return (y * g.astype(jnp.float32)).astype(x.dtype)


def layer_norm(x, g, b):
    x32 = x.astype(jnp.float32)
    mu = jnp.mean(x32, axis=-1, keepdims=True)
    xc = x32 - mu
    y = xc * lax.rsqrt(jnp.mean(xc * xc, axis=-1, keepdims=True) + EPS)
    return (y * g.astype(jnp.float32) + b.astype(jnp.float32)).astype(x.dtype)


def chunk_spatial_gating(u, v, w_s, b_s):
    bn, L, _ = v.shape
    pad = (-L) % A_CHUNK
    n = (L + pad) // A_CHUNK
    vp = jnp.pad(v, ((0, 0), (0, pad), (0, 0))).reshape(bn, n, A_CHUNK, A_HEADS, A_HEAD_DIM)
    mask = jnp.tril(jnp.ones((A_CHUNK, A_CHUNK), dtype=bool))
    w = jnp.where(mask, w_s, 0.0).astype(v.dtype)
    s = jnp.einsum('hts,bnshd->bnthd', w, vp) + b_s.T[:, :, None].astype(v.dtype)
    s = s.reshape(bn, n * A_CHUNK, A_WIDTH)[:, :L]
    return u * s


def hgrn2_recurrence(q, z_f, i_in, S0, lb):
    bn, L = q.shape[0], q.shape[1]
    z = z_f.astype(jnp.float32)
    lb = lb.astype(jnp.float32)
    logf = jnp.logaddexp(jnp.log(lb), jnp.log1p(-lb) + jax.nn.log_sigmoid(z))
    k = (1.0 - lb) * jax.nn.sigmoid(-z)
    qf = jax.nn.silu(q.astype(jnp.float32))
    vf = i_in.astype(jnp.float32)
    C = min(B_CHUNK, L)
    pad = (-L) % C
    n = (L + pad) // C

    def to_chunks(a):
        a = jnp.pad(a, ((0, 0), (0, pad), (0, 0), (0, 0)))
        return jnp.moveaxis(a.reshape(bn, n, C, a.shape[2], a.shape[3]), 1, 0)

    qc, lfc, kc, vc = to_chunks(qf), to_chunks(logf), to_chunks(k), to_chunks(vf)
    mask = jnp.tril(jnp.ones((C, C), dtype=bool))[None, :, :, None, None]

    def step(S, inp):
        qb, lfb, kb, vb = inp
        bcum = jnp.cumsum(lfb, axis=1)
        o_inter = jnp.einsum('bthk,bhkv->bthv', qb * jnp.exp(bcum), S)
        diff = bcum[:, :, None] - bcum[:, None, :]
        decay = jnp.exp(jnp.where(mask, diff, -jnp.inf))
        attn = jnp.einsum('bthk,btshk,bshk->btsh', qb, decay, kb)
        o_intra = jnp.einsum('btsh,bshv->bthv', attn, vb)
        blast = bcum[:, -1]
        S_new = jnp.exp(blast)[..., None] * S + jnp.einsum(
            'bshk,bshv->bhkv', kb * jnp.exp(blast[:, None] - bcum), vb)
        return S_new, o_inter + o_intra

    S_fin, o = lax.scan(step, S0.astype(jnp.float32), (qc, lfc, kc, vc))
    o = jnp.moveaxis(o, 0, 1).reshape(bn, n * C, B_HEADS, B_VAL_DIM)[:, :L]
    return o, S_fin


def trunk_layer(x, S0, lb, g_mix_pre, g_mix_post, w_in, ln_g, ln_b, w_s, b_s,
                g_out, w_out, g_ffn_pre, g_ffn_post, w_gate, w_up, w_down):
    bn, L, _ = x.shape
    h = rms_norm(x, g_mix_pre)
    proj = h @ w_in.astype(x.dtype)
    u, v, q, zf, ib, g = jnp.split(proj, SPLITS, axis=-1)
    u = jax.nn.gelu(u, approximate=False)
    v = layer_norm(jax.nn.gelu(v, approximate=False), ln_g, ln_b)
    o_a = chunk_spatial_gating(u, v, w_s, b_s)
    v_rows = v[:, ((L - 1) // A_CHUNK) * A_CHUNK:]
    o_b, S_new = hgrn2_recurrence(q.reshape(bn, L, B_HEADS, B_KEY_DIM),
                                  zf.reshape(bn, L, B_HEADS, B_KEY_DIM),
                                  ib.reshape(bn, L, B_HEADS, B_VAL_DIM), S0, lb)
    o_b = o_b * lax.rsqrt(jnp.mean(o_b * o_b, axis=-1, keepdims=True) + EPS)
    o_b = o_b * g_out.astype(jnp.float32).reshape(B_HEADS, B_VAL_DIM)
    o_b = (o_b.reshape(bn, L, B_WIDTH) * jax.nn.silu(g.astype(jnp.float32))).astype(x.dtype)
    mix = jnp.concatenate([o_a, o_b], axis=-1) @ w_out.astype(x.dtype)
    x = x + rms_norm(mix, g_mix_post)
    h = rms_norm(x, g_ffn_pre)
    f = (jax.nn.silu(h @ w_gate.astype(x.dtype)) * (h @ w_up.astype(x.dtype))) @ w_down.astype(x.dtype)
    x = x + rms_norm(f, g_ffn_post)
    return x, S_new, v_rows


def setup_inputs(seed: int = 0) -> dict:
    key = jax.random.key(seed)
    ks = jax.random.split(key, 20)
    nrm = jax.random.normal
    f32 = jnp.float32
    return {
        "x_prompt": nrm(ks[0], (BATCH, SEQ, D_MODEL), f32),
        "x_sample": nrm(ks[1], (DEC_BATCH, DEC_SEQ, D_MODEL), f32),
        "state_hgrn": nrm(ks[2], (DEPTH, DEC_BATCH, B_HEADS, B_KEY_DIM, B_VAL_DIM), f32),
        "norm_mix_pre": 1.0 + 0.05 * nrm(ks[3], (DEPTH, D_MODEL), f32),
        "norm_mix_post": 1.0 + 0.05 * nrm(ks[4], (DEPTH, D_MODEL), f32),
        "w_in": nrm(ks[5], (DEPTH, D_MODEL, IN_WIDTH), f32) * D_MODEL ** -0.5,
        "ln_v_gain": 1.0 + 0.05 * nrm(ks[6], (DEPTH, A_WIDTH), f32),
        "ln_v_bias": 0.02 * nrm(ks[7], (DEPTH, A_WIDTH), f32),
        "spatial_w": nrm(ks[8], (DEPTH, A_HEADS, A_CHUNK, A_CHUNK), f32) * A_CHUNK ** -0.5,
        "spatial_b": 1.0 + 0.1 * nrm(ks[9], (DEPTH, A_HEADS, A_CHUNK), f32),
        "lb_param": 0.5 * nrm(ks[10], (DEPTH, B_QF_WIDTH), f32),
        "hgrn_out_gain": 1.0 + 0.05 * nrm(ks[11], (DEPTH, B_WIDTH), f32),
        "w_out": nrm(ks[12], (DEPTH, MIX_WIDTH, D_MODEL), f32) * MIX_WIDTH ** -0.5,
        "norm_ffn_pre": 1.0 + 0.05 * nrm(ks[13], (DEPTH, D_MODEL), f32),
        "norm_ffn_post": 1.0 + 0.05 * nrm(ks[14], (DEPTH, D_MODEL), f32),
        "w_gate": nrm(ks[15], (DEPTH, D_MODEL, D_FF), f32) * D_MODEL ** -0.5,
        "w_up": nrm(ks[16], (DEPTH, D_MODEL, D_FF), f32) * D_MODEL ** -0.5,
        "w_down": nrm(ks[17], (DEPTH, D_FF, D_MODEL), f32) * D_FF ** -0.5,
    }


def reference(x_prompt, x_sample, state_hgrn, norm_mix_pre, norm_mix_post, w_in,
              ln_v_gain, ln_v_bias, spatial_w, spatial_b, lb_param, hgrn_out_gain,
              w_out, norm_ffn_pre, norm_ffn_post, w_gate, w_up, w_down):
    lb_cum = jnp.cumsum(jax.nn.softmax(lb_param.astype(jnp.float32), axis=0), axis=0)
    lb_all = (lb_cum - lb_cum[0:1]).reshape(DEPTH, B_HEADS, B_KEY_DIM)

    def run(x, S_in):
        S_out, v_out = [], []
        for l in range(DEPTH):
            x, S_l, v_l = trunk_layer(
                x, S_in[l], lb_all[l], norm_mix_pre[l], norm_mix_post[l], w_in[l],
                ln_v_gain[l], ln_v_bias[l], spatial_w[l], spatial_b[l], hgrn_out_gain[l],
                w_out[l], norm_ffn_pre[l], norm_ffn_post[l], w_gate[l], w_up[l], w_down[l])
            S_out.append(S_l.astype(S_in.dtype))
            v_out.append(v_l)
        return x, jnp.stack(S_out), jnp.stack(v_out)

    S_zero = jnp.zeros((DEPTH, BATCH, B_HEADS, B_KEY_DIM, B_VAL_DIM), state_hgrn.dtype)
    y_prompt, state_hgrn_prompt, state_v_prompt = run(x_prompt, S_zero)
    y_sample, state_hgrn_sample, state_v_sample = run(x_sample, state_hgrn)
    return (y_prompt, y_sample, state_hgrn_prompt, state_hgrn_sample, state_v_prompt, state_v_sample)
```

```python
import functools

import numpy as np
import jax
import jax.numpy as jnp
from jax import lax
from jax.experimental import pallas as pl
from jax.experimental.pallas import tpu as pltpu

F32 = jnp.float32
BF16 = jnp.bfloat16

EPS = 1e-6
LANES = 128
SUBLANES = 8
SEG = 1024
HEADS = 8
HEAD_DIM = SEG // HEADS
CHUNK = 128
N_LEVELS = 7
VMEM_LIMIT = 56 * 1024 * 1024


def _rms_norm(x, g):
    return x * lax.rsqrt(jnp.mean(x * x, axis=-1, keepdims=True) + EPS) * g


def _sigmoid(x):
    return 1.0 / (1.0 + jnp.exp(-x))


def _gelu(x):
    return 0.5 * x * (1.0 + lax.erf(x * np.float32(np.sqrt(0.5))))


def _dot(a, b):
    return jnp.dot(a, b, preferred_element_type=F32)


def _dot_nt(a, b):
    return lax.dot_general(a, b, (((1,), (1,)), ((), ())), preferred_element_type=F32)


def _dot_tn(a, b):
    return lax.dot_general(a, b, (((0,), (0,)), ((), ())), preferred_element_type=F32)


def _forget_lower_bound(lbp, layer):
    m = jnp.max(lbp, axis=0, keepdims=True)
    e = jnp.exp(lbp - m)
    sm = e / jnp.sum(e, axis=0, keepdims=True)
    cum = sm[0:1]
    for r in range(1, layer + 1):
        cum = cum + sm[r:r + 1]
    return cum - sm[0:1]


def _inproj_kernel(x_ref, gpre_ref, w_ref, lng_ref, lnb_ref, ws_ref, bs_ref, lbp_ref,
                   oa_ref, vrows_ref, qs_ref, lf_ref, kk_ref, ib_ref, gs_ref,
                   h_scr, u_scr, *, layer, decode):
    j = pl.program_id(1)

    @pl.when(j == 0)
    def _():
        h_scr[...] = _rms_norm(x_ref[...], gpre_ref[...]).astype(BF16)

    acc = _dot(h_scr[...], w_ref[...])

    @pl.when(j == 0)
    def _():
        u_scr[...] = _gelu(acc)

    @pl.when(j == 1)
    def _():
        gv = _gelu(acc)
        mu = jnp.mean(gv, axis=-1, keepdims=True)
        vc = gv - mu
        v = vc * lax.rsqrt(jnp.mean(vc * vc, axis=-1, keepdims=True) + EPS)
        v = v * lng_ref[...] + lnb_ref[...]
        tm = v.shape[0]
        if decode:
            vrows_ref[...] = v
            oa_ref[...] = (u_scr[...] * (v * ws_ref[...] + bs_ref[...])).astype(BF16)
        else:
            vrows_ref[...] = v[tm - CHUNK:, :]
            vb = v.astype(BF16)
            row = lax.broadcasted_iota(jnp.int32, (CHUNK, CHUNK), 0)
            col = lax.broadcasted_iota(jnp.int32, (CHUNK, CHUNK), 1)
            causal = row >= col
            for hh in range(HEADS):
                cs = slice(hh * HEAD_DIM, (hh + 1) * HEAD_DIM)
                w_h = jnp.where(causal, ws_ref[hh], 0.0).astype(BF16)
                b_h = bs_ref[:, hh:hh + 1]
                for c in range(tm // CHUNK):
                    rs = slice(c * CHUNK, (c + 1) * CHUNK)
                    s = _dot(w_h, vb[rs, cs]) + b_h
                    oa_ref[rs, cs] = (u_scr[rs, cs] * s).astype(BF16)

    @pl.when(j == 2)
    def _():
        qs_ref[...] = (acc * _sigmoid(acc)).astype(BF16)

    @pl.when(j == 3)
    def _():
        lb = _forget_lower_bound(lbp_ref[...], layer)
        log_lb = jnp.log(lb)
        log_1m = jnp.log1p(-lb)
        z = acc
        ez = jnp.exp(-jnp.abs(z))
        logsig = jnp.minimum(z, 0.0) - jnp.log1p(ez)
        b2 = log_1m + logsig
        amax = jnp.maximum(log_lb, b2)
        lf_ref[...] = amax + jnp.log1p(jnp.exp(-jnp.abs(log_lb - b2)))
        kk_ref[...] = ((1.0 - lb) * _sigmoid(-z)).astype(BF16)

    @pl.when(j == 4)
    def _():
        ib_ref[...] = acc.astype(BF16)

    @pl.when(j == 5)
    def _():
        gs_ref[...] = (acc * _sigmoid(acc)).astype(BF16)


def _inproj(x, gpre, w, lng, lnb, ws, bs, lbp, *, layer, decode, tm, seq_len):
    m, d = x.shape
    n_seg = w.shape[1] // SEG
    if decode:
        vrows_shape, vrows_spec = (m, SEG), pl.BlockSpec((tm, SEG), lambda i, j: (i, 0))
        ws_spec = pl.BlockSpec((1, SEG), lambda i, j: (0, 0))
        bs_spec = pl.BlockSpec((1, SEG), lambda i, j: (0, 0))
    else:
        tiles_per_seq = seq_len // tm
        vrows_shape = (m // seq_len * CHUNK, SEG)
        vrows_spec = pl.BlockSpec((CHUNK, SEG), lambda i, j: (i // tiles_per_seq, 0))
        ws_spec = pl.BlockSpec((HEADS, CHUNK, CHUNK), lambda i, j: (0, 0, 0))
        bs_spec = pl.BlockSpec((CHUNK, HEADS), lambda i, j: (0, 0))
    row_spec = pl.BlockSpec((tm, SEG), lambda i, j: (i, 0))
    vec_spec = pl.BlockSpec((1, SEG), lambda i, j: (0, 0))
    out_shape = [
        jax.ShapeDtypeStruct((m, SEG), BF16),
        jax.ShapeDtypeStruct(vrows_shape, F32),
        jax.ShapeDtypeStruct((m, SEG), BF16),
        jax.ShapeDtypeStruct((m, SEG), F32),
        jax.ShapeDtypeStruct((m, SEG), BF16),
        jax.ShapeDtypeStruct((m, SEG), BF16),
        jax.ShapeDtypeStruct((m, SEG), BF16),
    ]
    out_specs = [row_spec, vrows_spec, row_spec, row_spec, row_spec, row_spec, row_spec]
    return pl.pallas_call(
        functools.partial(_inproj_kernel, layer=layer, decode=decode),
        grid=(m // tm, n_seg),
        in_specs=[
            pl.BlockSpec((tm, d), lambda i, j: (i, 0)),
            pl.BlockSpec((1, d), lambda i, j: (0, 0)),
            pl.BlockSpec((d, SEG), lambda i, j: (0, j)),
            vec_spec, vec_spec, ws_spec, bs_spec,
            pl.BlockSpec(lbp.shape, lambda i, j: (0, 0)),
        ],
        out_specs=out_specs,
        out_shape=out_shape,
        scratch_shapes=[pltpu.VMEM((tm, d), BF16), pltpu.VMEM((tm, SEG), F32)],
        compiler_params=pltpu.CompilerParams(
            dimension_semantics=("arbitrary", "arbitrary"), vmem_limit_bytes=VMEM_LIMIT),
        name="inproj_decode" if decode else "inproj_prompt",
    )(x, gpre, w, lng, lnb, ws, bs, lbp)


def _level_table():
    t = np.arange(CHUNK)[:, None]
    s = np.arange(CHUNK)[None, :]
    x = t ^ s
    lev = np.floor(np.log2(np.maximum(x, 1))).astype(np.int32)
    lev = np.where(t == s, -1, lev)
    lev = np.where(t < s, -2, lev)
    return lev.astype(np.int32)


def _block_reference(b, level):
    half = 1 << level
    blk = 2 * half
    k = b.shape[1]
    if blk >= 2 * SUBLANES:
        bb = b.reshape(CHUNK // blk, blk, k)
        ref = jnp.broadcast_to(bb[:, half - 1:half, :], bb.shape)
        return ref.reshape(CHUNK, k)
    b3 = b.reshape(CHUNK // SUBLANES, SUBLANES, k)
    sub = lax.broadcasted_iota(jnp.int32, b3.shape, 1)
    ref = None
    for start in range(0, SUBLANES, blk):
        cand = jnp.broadcast_to(b3[:, start + half - 1:start + half, :], b3.shape)
        ref = cand if ref is None else jnp.where(sub >= start, cand, ref)
    return ref.reshape(CHUNK, k)


def _hgrn_prompt_kernel(qs_ref, lf_ref, kk_ref, ib_ref, gs_ref, gout_ref, s0t_ref, lev_ref,
                        ob_ref, sfin_ref, st_scr):
    c = pl.program_id(1)

    @pl.when(c == 0)
    def _():
        st_scr[...] = s0t_ref[0]

    lev = lev_ref[...]
    row = lax.broadcasted_iota(jnp.int32, (CHUNK, CHUNK), 0)
    col = lax.broadcasted_iota(jnp.int32, (CHUNK, CHUNK), 1)
    tri = jnp.where(row >= col, 1.0, 0.0).astype(BF16)

    for hh in range(HEADS):
        cs = slice(hh * HEAD_DIM, (hh + 1) * HEAD_DIM)
        lf = lf_ref[:, cs]
        q = qs_ref[:, cs].astype(F32)
        kk = kk_ref[:, cs].astype(F32)
        v = ib_ref[:, cs]
        st = st_scr[hh]

        hi = lf.astype(BF16)
        r1 = lf - hi.astype(F32)
        mid = r1.astype(BF16)
        lo = (r1 - mid.astype(F32)).astype(BF16)
        b = _dot(tri, hi) + _dot(tri, mid) + _dot(tri, lo)
        b_last = b[CHUNK - 1:CHUNK, :]

        o = _dot_nt((q * jnp.exp(b)).astype(BF16), st.astype(BF16))

        a = jnp.where(lev == -1, _dot_nt(q.astype(BF16), kk.astype(BF16)), 0.0)
        for level in range(N_LEVELS):
            e = jnp.exp(-jnp.abs(b - _block_reference(b, level)))
            m = _dot_nt((q * e).astype(BF16), (kk * e).astype(BF16))
            a = jnp.where(lev == level, m, a)
        o = o + _dot(a.astype(BF16), v)

        kd = (kk * jnp.exp(b_last - b)).astype(BF16)
        st_new = st * jnp.exp(b_last) + _dot_tn(v, kd)
        st_scr[hh] = st_new

        on = o * lax.rsqrt(jnp.mean(o * o, axis=-1, keepdims=True) + EPS)
        ob_ref[:, cs] = (on * gout_ref[:, cs] * gs_ref[:, cs].astype(F32)).astype(BF16)

        @pl.when(c == pl.num_programs(1) - 1)
        def _():
            sfin_ref[0, hh] = st_new.T


def _hgrn_prompt(qs, lf, kk, ib, gs, gout, s0t, lev, *, batch, seq_len):
    m = qs.shape[0]
    nc = seq_len // CHUNK
    row_spec = pl.BlockSpec((CHUNK, SEG), lambda b, c: (b * nc + c, 0))
    state_spec = pl.BlockSpec((1, HEADS, HEAD_DIM, HEAD_DIM), lambda b, c: (b, 0, 0, 0))
    return pl.pallas_call(
        _hgrn_prompt_kernel,
        grid=(batch, nc),
        in_specs=[row_spec, row_spec, row_spec, row_spec, row_spec,
                  pl.BlockSpec((1, SEG), lambda b, c: (0, 0)),
                  state_spec,
                  pl.BlockSpec((CHUNK, CHUNK), lambda b, c: (0, 0))],
        out_specs=[row_spec, state_spec],
        out_shape=[jax.ShapeDtypeStruct((m, SEG), BF16),
                   jax.ShapeDtypeStruct((batch, HEADS, HEAD_DIM, HEAD_DIM), F32)],
        scratch_shapes=[pltpu.VMEM((HEADS, HEAD_DIM, HEAD_DIM), F32)],
        compiler_params=pltpu.CompilerParams(
            dimension_semantics=("arbitrary", "arbitrary"), vmem_limit_bytes=VMEM_LIMIT),
        name="hgrn_prompt",
    )(qs, lf, kk, ib, gs, gout, s0t, lev)


def _hgrn_decode_kernel(lft_ref, kkt_ref, qst_ref, ib_ref, gs_ref, gout_ref, s0_ref,
                        ob_ref, snew_ref, o_scr):
    tb = s0_ref.shape[0]
    for bb in range(tb):
        for hh in range(HEADS):
            ks = slice(hh * HEAD_DIM, (hh + 1) * HEAD_DIM)
            f_col = jnp.exp(lft_ref[0, ks, bb:bb + 1])
            k_col = kkt_ref[0, ks, bb:bb + 1].astype(F32)
            q_col = qst_ref[0, ks, bb:bb + 1].astype(F32)
            i_row = ib_ref[bb:bb + 1, ks].astype(F32)
            s_new = f_col * s0_ref[bb, hh] + k_col * i_row
            snew_ref[bb, hh] = s_new
            o_scr[bb:bb + 1, ks] = jnp.sum(q_col * s_new, axis=0, keepdims=True)
    for hh in range(HEADS):
        ks = slice(hh * HEAD_DIM, (hh + 1) * HEAD_DIM)
        o = o_scr[:, ks]
        on = o * lax.rsqrt(jnp.mean(o * o, axis=-1, keepdims=True) + EPS)
        ob_ref[:, ks] = (on * gout_ref[:, ks] * gs_ref[:, ks].astype(F32)).astype(BF16)


def _hgrn_decode(lft, kkt, qst, ib, gs, gout, s0, *, tb):
    m = ib.shape[0]
    col_spec = pl.BlockSpec((1, SEG, tb), lambda i: (i, 0, 0))
    row_spec = pl.BlockSpec((tb, SEG), lambda i: (i, 0))
    state_spec = pl.BlockSpec((tb, HEADS, HEAD_DIM, HEAD_DIM), lambda i: (i, 0, 0, 0))
    return pl.pallas_call(
        _hgrn_decode_kernel,
        grid=(m // tb,),
        in_specs=[col_spec, col_spec, col_spec, row_spec, row_spec,
                  pl.BlockSpec((1, SEG), lambda i: (0, 0)), state_spec],
        out_specs=[row_spec, state_spec],
        out_shape=[jax.ShapeDtypeStruct((m, SEG), BF16),
                   jax.ShapeDtypeStruct(s0.shape, F32)],
        scratch_shapes=[pltpu.VMEM((tb, SEG), F32)],
        compiler_params=pltpu.CompilerParams(
            dimension_semantics=("arbitrary",), vmem_limit_bytes=VMEM_LIMIT),
        name="hgrn_decode",
    )(lft, kkt, qst, ib, gs, gout, s0)


def _outproj_kernel(oa_ref, ob_ref, w_ref, x_ref, g_ref, o_ref):
    mix = _dot(oa_ref[...], w_ref[:SEG, :]) + _dot(ob_ref[...], w_ref[SEG:, :])
    o_ref[...] = x_ref[...] + _rms_norm(mix, g_ref[...])


def _outproj(oa, ob, w, x, g, *, tm):
    m, d = x.shape
    return pl.pallas_call(
        _outproj_kernel,
        grid=(m // tm,),
        in_specs=[pl.BlockSpec((tm, SEG), lambda i: (i, 0)),
                  pl.BlockSpec((tm, SEG), lambda i: (i, 0)),
                  pl.BlockSpec(w.shape, lambda i: (0, 0)),
                  pl.BlockSpec((tm, d), lambda i: (i, 0)),
                  pl.BlockSpec((1, d), lambda i: (0, 0))],
        out_specs=pl.BlockSpec((tm, d), lambda i: (i, 0)),
        out_shape=jax.ShapeDtypeStruct((m, d), F32),
        compiler_params=pltpu.CompilerParams(
            dimension_semantics=("arbitrary",), vmem_limit_bytes=VMEM_LIMIT),
        name="outproj",
    )(oa, ob, w, x, g)


def _ffn_kernel(x_ref, gpre_ref, wg_ref, wu_ref, wd_ref, gpost_ref, o_ref, h_scr, acc_scr):
    f = pl.program_id(1)

    @pl.when(f == 0)
    def _():
        h_scr[...] = _rms_norm(x_ref[...], gpre_ref[...]).astype(BF16)
        acc_scr[...] = jnp.zeros_like(acc_scr)

    h = h_scr[...]
    gate = _dot(h, wg_ref[...])
    up = _dot(h, wu_ref[...])
    t = (gate * _sigmoid(gate) * up).astype(BF16)
    acc_scr[...] += _dot(t, wd_ref[...])

    @pl.when(f == pl.num_programs(1) - 1)
    def _():
        o_ref[...] = x_ref[...] + _rms_norm(acc_scr[...], gpost_ref[...])


def _ffn(x, gpre, wg, wu, wd, gpost, *, tm, tf):
    m, d = x.shape
    dff = wg.shape[1]
    return pl.pallas_call(
        _ffn_kernel,
        grid=(m // tm, dff // tf),
        in_specs=[pl.BlockSpec((tm, d), lambda i, f: (i, 0)),
                  pl.BlockSpec((1, d), lambda i, f: (0, 0)),
                  pl.BlockSpec((d, tf), lambda i, f: (0, f)),
                  pl.BlockSpec((d, tf), lambda i, f: (0, f)),
                  pl.BlockSpec((tf, d), lambda i, f: (f, 0)),
                  pl.BlockSpec((1, d), lambda i, f: (0, 0))],
        out_specs=pl.BlockSpec((tm, d), lambda i, f: (i, 0)),
        out_shape=jax.ShapeDtypeStruct((m, d), F32),
        scratch_shapes=[pltpu.VMEM((tm, d), BF16), pltpu.VMEM((tm, d), F32)],
        compiler_params=pltpu.CompilerParams(
            dimension_semantics=("arbitrary", "arbitrary"), vmem_limit_bytes=VMEM_LIMIT),
        name="ffn",
    )(x, gpre, wg, wu, wd, gpost)


PROMPT_TM = 512
FFN_TF = 512
DECODE_TB = 8


def kernel(x_prompt, x_sample, state_hgrn, norm_mix_pre, norm_mix_post, w_in, ln_v_gain, ln_v_bias,
           spatial_w, spatial_b, lb_param, hgrn_out_gain, w_out, norm_ffn_pre, norm_ffn_post,
           w_gate, w_up, w_down):
    depth = w_in.shape[0]
    batch, seq_len, d = x_prompt.shape
    dec_batch, dec_seq, _ = x_sample.shape
    assert dec_seq == 1 and seq_len % PROMPT_TM == 0 and PROMPT_TM % CHUNK == 0
    assert dec_batch % DECODE_TB == 0 and w_in.shape[2] == 6 * SEG and w_out.shape[1] == 2 * SEG

    w_in_b = w_in.astype(BF16)
    w_out_b = w_out.astype(BF16)
    w_gate_b = w_gate.astype(BF16)
    w_up_b = w_up.astype(BF16)
    w_down_b = w_down.astype(BF16)
    lev = jnp.asarray(_level_table())
    row = lambda a: a.reshape(1, -1)

    xp = x_prompt.reshape(batch * seq_len, d)
    xs = x_sample.reshape(dec_batch, d)
    s0t_zero = jnp.zeros((batch, HEADS, HEAD_DIM, HEAD_DIM), F32)
    sp_out, ss_out, vp_out, vs_out = [], [], [], []

    for l in range(depth):
        common = (row(ln_v_gain[l]), row(ln_v_bias[l]))
        oa, vrows, qs, lf, kk, ib, gs = _inproj(
            xp, row(norm_mix_pre[l]), w_in_b[l], *common, spatial_w[l], spatial_b[l].T, lb_param,
            layer=l, decode=False, tm=PROMPT_TM, seq_len=seq_len)
        ob, s_fin = _hgrn_prompt(qs, lf, kk, ib, gs, row(hgrn_out_gain[l]), s0t_zero, lev,
                                 batch=batch, seq_len=seq_len)
        xp = _outproj(oa, ob, w_out_b[l], xp, row(norm_mix_post[l]), tm=PROMPT_TM)
        xp = _ffn(xp, row(norm_ffn_pre[l]), w_gate_b[l], w_up_b[l], w_down_b[l],
                  row(norm_ffn_post[l]), tm=PROMPT_TM, tf=FFN_TF)
        sp_out.append(s_fin)
        vp_out.append(vrows.reshape(batch, CHUNK, SEG))
        w00 = row(jnp.repeat(spatial_w[l, :, 0, 0], HEAD_DIM))
        b0 = row(jnp.repeat(spatial_b[l, :, 0], HEAD_DIM))
        oa, vrows, qs, lf, kk, ib, gs = _inproj(
            xs, row(norm_mix_pre[l]), w_in_b[l], *common, w00, b0, lb_param,
            layer=l, decode=True, tm=dec_batch, seq_len=1)
        cols = lambda a: a.reshape(dec_batch // DECODE_TB, DECODE_TB, SEG).transpose(0, 2, 1)
        ob, s_new = _hgrn_decode(cols(lf), cols(kk), cols(qs), ib, gs, row(hgrn_out_gain[l]),
                                 state_hgrn[l], tb=DECODE_TB)
        xs = _outproj(oa, ob, w_out_b[l], xs, row(norm_mix_post[l]), tm=dec_batch)
        xs = _ffn(xs, row(norm_ffn_pre[l]), w_gate_b[l], w_up_b[l], w_down_b[l],
                  row(norm_ffn_post[l]), tm=dec_batch, tf=FFN_TF)
        ss_out.append(s_new)
        vs_out.append(vrows.reshape(dec_batch, 1, SEG))

    return (xp.reshape(batch, seq_len, d), xs.reshape(dec_batch, 1, d),
            jnp.stack(sp_out), jnp.stack(ss_out), jnp.stack(vp_out), jnp.stack(vs_out))
```

```python
import functools

import numpy as np
import jax
import jax.numpy as jnp
from jax import lax
from jax.experimental import pallas as pl
from jax.experimental.pallas import tpu as pltpu

F32 = jnp.float32
BF16 = jnp.bfloat16

EPS = 1e-6
LANES = 128
SUBLANES = 8
SEG = 1024
HEADS = 8
HEAD_DIM = SEG // HEADS
CHUNK = 128
N_LEVELS = 7
ROW_CHUNK = 256
LOG2E = np.float32(1.4426950408889634)
VMEM_LIMIT = 56 * 1024 * 1024


def _rms_norm(x, g):
    return x * lax.rsqrt(jnp.mean(x * x, axis=-1, keepdims=True) + EPS) * g


def _sigmoid(x):
    return 1.0 / (1.0 + jnp.exp(-x))


def _gelu(x):
    return 0.5 * x * (1.0 + lax.erf(x * np.float32(np.sqrt(0.5))))


def _dot(a, b):
    return jnp.dot(a, b, preferred_element_type=F32)


def _dot_nt(a, b):
    return lax.dot_general(a, b, (((1,), (1,)), ((), ())), preferred_element_type=F32)


def _dot_tn(a, b):
    return lax.dot_general(a, b, (((0,), (0,)), ((), ())), preferred_element_type=F32)


def _forget_lower_bound(lbp, layer):
    m = jnp.max(lbp, axis=0, keepdims=True)
    e = jnp.exp(lbp - m)
    sm = e / jnp.sum(e, axis=0, keepdims=True)
    cum = sm[0:1]
    for r in range(1, layer + 1):
        cum = cum + sm[r:r + 1]
    return cum - sm[0:1]


def _layer_vec_spec(layer, width, grid_rank):
    if grid_rank == 1:
        return pl.BlockSpec((None, 1, width), lambda i: (layer, 0, 0))
    return pl.BlockSpec((None, 1, width), lambda i, j: (layer, 0, 0))


def _inproj_kernel(x_ref, gpre_ref, w_ref, lng_ref, lnb_ref, ws_ref, bs_ref, lbp_ref,
                   oa_ref, vrows_ref, qs_ref, lf_ref, kk_ref, ib_ref, gs_ref,
                   h_scr, u_scr, *, layer, decode):
    j = pl.program_id(1)
    tm = x_ref.shape[0]
    rc = min(tm, ROW_CHUNK)

    def segment(epilogue, prologue=None):
        for r in range(tm // rc):
            rows = slice(r * rc, (r + 1) * rc)
            if prologue is not None:
                prologue(rows)
            epilogue(_dot(h_scr[rows, :], w_ref[...]), rows, r)

    @pl.when(j == 0)
    def _():
        def norm_rows(rows):
            h_scr[rows, :] = _rms_norm(x_ref[rows, :], gpre_ref[...]).astype(BF16)

        def epilogue(acc, rows, r):
            u_scr[rows, :] = _gelu(acc)
        segment(epilogue, norm_rows)

    @pl.when(j == 1)
    def _():
        if not decode:
            row = lax.broadcasted_iota(jnp.int32, (CHUNK, CHUNK), 0)
            col = lax.broadcasted_iota(jnp.int32, (CHUNK, CHUNK), 1)
            causal = row >= col

        def epilogue(acc, rows, r):
            gv = _gelu(acc)
            mu = jnp.mean(gv, axis=-1, keepdims=True)
            vc = gv - mu
            v = vc * lax.rsqrt(jnp.mean(vc * vc, axis=-1, keepdims=True) + EPS)
            v = v * lng_ref[...] + lnb_ref[...]
            if decode:
                vrows_ref[rows, :] = v
                oa_ref[rows, :] = (u_scr[rows, :] * (v * ws_ref[...] + bs_ref[...])).astype(BF16)
                return
            if r == tm // rc - 1:
                vrows_ref[...] = v[rc - CHUNK:, :]
            vb = v.astype(BF16)
            for hh in range(HEADS):
                cs = slice(hh * HEAD_DIM, (hh + 1) * HEAD_DIM)
                w_h = jnp.where(causal, ws_ref[hh], 0.0).astype(BF16)
                b_h = bs_ref[:, hh:hh + 1]
                for c in range(rc // CHUNK):
                    s = _dot(w_h, vb[c * CHUNK:(c + 1) * CHUNK, cs]) + b_h
                    rs = slice(rows.start + c * CHUNK, rows.start + (c + 1) * CHUNK)
                    oa_ref[rs, cs] = (u_scr[rs, cs] * s).astype(BF16)
        segment(epilogue)

    @pl.when(j == 2)
    def _():
        def epilogue(acc, rows, r):
            qs_ref[rows, :] = (acc * _sigmoid(acc)).astype(BF16)
        segment(epilogue)

    @pl.when(j == 3)
    def _():
        lb = _forget_lower_bound(lbp_ref[...], layer)
        log_lb = jnp.log(lb)
        log_1m = jnp.log1p(-lb)

        def epilogue(z, rows, r):
            ez = jnp.exp(-jnp.abs(z))
            logsig = jnp.minimum(z, 0.0) - jnp.log1p(ez)
            b2 = log_1m + logsig
            amax = jnp.maximum(log_lb, b2)
            lf_ref[rows, :] = amax + jnp.log1p(jnp.exp(-jnp.abs(log_lb - b2)))
            kk_ref[rows, :] = ((1.0 - lb) * _sigmoid(-z)).astype(BF16)
        segment(epilogue)

    @pl.when(j == 4)
    def _():
        def epilogue(acc, rows, r):
            ib_ref[rows, :] = acc.astype(BF16)
        segment(epilogue)

    @pl.when(j == 5)
    def _():
        def epilogue(acc, rows, r):
            gs_ref[rows, :] = (acc * _sigmoid(acc)).astype(BF16)
        segment(epilogue)


def _inproj(x, gpre, w, lng, lnb, ws, bs, lbp, *, layer, decode, tm, seq_len):
    m, d = x.shape
    n_seg = w.shape[2] // SEG
    if decode:
        vrows_shape, vrows_spec = (m, SEG), pl.BlockSpec((tm, SEG), lambda i, j: (i, 0))
        ws_spec = _layer_vec_spec(layer, SEG, 2)
        bs_spec = _layer_vec_spec(layer, SEG, 2)
    else:
        tiles_per_seq = seq_len // tm
        vrows_shape = (m // seq_len * CHUNK, SEG)
        vrows_spec = pl.BlockSpec((CHUNK, SEG), lambda i, j: (i // tiles_per_seq, 0))
        ws_spec = pl.BlockSpec((None, HEADS, CHUNK, CHUNK), lambda i, j: (layer, 0, 0, 0))
        bs_spec = pl.BlockSpec((None, CHUNK, HEADS), lambda i, j: (layer, 0, 0))
    row_spec = pl.BlockSpec((tm, SEG), lambda i, j: (i, 0))
    out_shape = [
        jax.ShapeDtypeStruct((m, SEG), BF16),
        jax.ShapeDtypeStruct(vrows_shape, F32),
        jax.ShapeDtypeStruct((m, SEG), BF16),
        jax.ShapeDtypeStruct((m, SEG), F32),
        jax.ShapeDtypeStruct((m, SEG), BF16),
        jax.ShapeDtypeStruct((m, SEG), BF16),
        jax.ShapeDtypeStruct((m, SEG), BF16),
    ]
    out_specs = [row_spec, vrows_spec, row_spec, row_spec, row_spec, row_spec, row_spec]
    return pl.pallas_call(
        functools.partial(_inproj_kernel, layer=layer, decode=decode),
        grid=(m // tm, n_seg),
        in_specs=[
            pl.BlockSpec((tm, d), lambda i, j: (i, 0)),
            _layer_vec_spec(layer, d, 2),
            pl.BlockSpec((None, d, SEG), lambda i, j: (layer, 0, j)),
            _layer_vec_spec(layer, SEG, 2), _layer_vec_spec(layer, SEG, 2), ws_spec, bs_spec,
            pl.BlockSpec(lbp.shape, lambda i, j: (0, 0)),
        ],
        out_specs=out_specs,
        out_shape=out_shape,
        scratch_shapes=[pltpu.VMEM((tm, d), BF16), pltpu.VMEM((tm, SEG), F32)],
        compiler_params=pltpu.CompilerParams(
            dimension_semantics=("arbitrary", "arbitrary"), vmem_limit_bytes=VMEM_LIMIT),
        name="inproj_decode" if decode else "inproj_prompt",
    )(x, gpre, w, lng, lnb, ws, bs, lbp)


def _level_table():
    t = np.arange(CHUNK)[:, None]
    s = np.arange(CHUNK)[None, :]
    x = t ^ s
    lev = np.floor(np.log2(np.maximum(x, 1))).astype(np.int32)
    lev = np.where(t == s, -1, lev)
    lev = np.where(t < s, -2, lev)
    return lev.astype(np.int32)


def _log_gap_to_block_boundary(b, level):
    half = 1 << level
    blk = 2 * half
    k = b.shape[1]
    if half >= SUBLANES:
        bb = b.reshape(CHUNK // blk, blk, k)
        r = bb[:, half - 1:half, :]
        gap = jnp.concatenate([r - bb[:, :half, :], bb[:, half:, :] - r], axis=1)
        return gap.reshape(CHUNK, k)
    b3 = b.reshape(CHUNK // SUBLANES, SUBLANES, k)
    sub = lax.broadcasted_iota(jnp.int32, (1, SUBLANES, k), 1)
    r = None
    for start in range(0, SUBLANES, blk):
        cand = jnp.broadcast_to(b3[:, start + half - 1:start + half, :], b3.shape)
        r = cand if r is None else jnp.where(sub >= start, cand, r)
    sign = jnp.where((sub & half) != 0, 1.0, -1.0)
    return ((b3 - r) * sign).reshape(CHUNK, k)


def _hgrn_prompt_kernel(qs_ref, lf_ref, kk_ref, ib_ref, gs_ref, gout_ref, s0t_ref, lev_ref,
                        ob_ref, sfin_ref, st_scr):
    c = pl.program_id(1)

    @pl.when(c == 0)
    def _():
        st_scr[...] = s0t_ref[0]

    lev = lev_ref[...]
    row = lax.broadcasted_iota(jnp.int32, (CHUNK, CHUNK), 0)
    col = lax.broadcasted_iota(jnp.int32, (CHUNK, CHUNK), 1)
    tri = jnp.where(row >= col, 1.0, 0.0).astype(BF16)

    odd_row = (lax.broadcasted_iota(jnp.int32, (CHUNK, HEAD_DIM), 0) & 1) == 1
    for hh in range(HEADS):
        cs = slice(hh * HEAD_DIM, (hh + 1) * HEAD_DIM)
        lf2 = lf_ref[:, cs] * LOG2E
        qb = qs_ref[:, cs]
        kb = kk_ref[:, cs]
        q = qb.astype(F32)
        kk = kb.astype(F32)
        v = ib_ref[:, cs]
        st = st_scr[hh]

        hi = lf2.astype(BF16)
        r1 = lf2 - hi.astype(F32)
        mid = r1.astype(BF16)
        lo = (r1 - mid.astype(F32)).astype(BF16)
        b = _dot(tri, hi) + _dot(tri, mid) + _dot(tri, lo)
        b_last = b[CHUNK - 1:CHUNK, :]

        o = _dot_nt((q * jnp.exp2(b)).astype(BF16), st.astype(BF16))

        a = jnp.where(lev == -1, _dot_nt(qb, kb), 0.0)
        for level in range(N_LEVELS):
            if level == 0:
                d = jnp.where(odd_row, lf2, 0.0)
            else:
                d = _log_gap_to_block_boundary(b, level)
            e = jnp.exp2(d)
            m = _dot_nt((q * e).astype(BF16), (kk * e).astype(BF16))
            a = jnp.where(lev == level, m, a)
        o = o + _dot(a.astype(BF16), v)

        kd = (kk * jnp.exp2(b_last - b)).astype(BF16)
        st_scr[hh] = st * jnp.exp2(b_last) + _dot_tn(v, kd)

        on = o * lax.rsqrt(jnp.mean(o * o, axis=-1, keepdims=True) + EPS)
        ob_ref[:, cs] = (on * gout_ref[:, cs] * gs_ref[:, cs].astype(F32)).astype(BF16)

    @pl.when(c == pl.num_programs(1) - 1)
    def _():
        for hh in range(HEADS):
            sfin_ref[0, hh] = st_scr[hh].T


def _hgrn_prompt(qs, lf, kk, ib, gs, gout, s0t, lev, *, layer, batch, seq_len):
    m = qs.shape[0]
    nc = seq_len // CHUNK
    row_spec = pl.BlockSpec((CHUNK, SEG), lambda b, c: (b * nc + c, 0))
    state_spec = pl.BlockSpec((1, HEADS, HEAD_DIM, HEAD_DIM), lambda b, c: (b, 0, 0, 0))
    return pl.pallas_call(
        _hgrn_prompt_kernel,
        grid=(batch, nc),
        in_specs=[row_spec, row_spec, row_spec, row_spec, row_spec,
                  _layer_vec_spec(layer, SEG, 2),
                  state_spec,
                  pl.BlockSpec((CHUNK, CHUNK), lambda b, c: (0, 0))],
        out_specs=[row_spec, state_spec],
        out_shape=[jax.ShapeDtypeStruct((m, SEG), BF16),
                   jax.ShapeDtypeStruct((batch, HEADS, HEAD_DIM, HEAD_DIM), F32)],
        scratch_shapes=[pltpu.VMEM((HEADS, HEAD_DIM, HEAD_DIM), F32)],
        compiler_params=pltpu.CompilerParams(
            dimension_semantics=("arbitrary", "arbitrary"), vmem_limit_bytes=VMEM_LIMIT),
        name="hgrn_prompt",
    )(qs, lf, kk, ib, gs, gout, s0t, lev)


def _hgrn_decode_kernel(lft_ref, kkt_ref, qst_ref, ib_ref, gs_ref, gout_ref, s0_ref, *rest):
    ob_ref, snew_ref, o_scr = rest[-3:]
    tb = s0_ref.shape[0]
    for bb in range(tb):
        for hh in range(HEADS):
            ks = slice(hh * HEAD_DIM, (hh + 1) * HEAD_DIM)
            f_col = jnp.exp(lft_ref[0, ks, bb:bb + 1])
            k_col = kkt_ref[0, ks, bb:bb + 1].astype(F32)
            q_col = qst_ref[0, ks, bb:bb + 1].astype(F32)
            i_row = ib_ref[bb:bb + 1, ks].astype(F32)
            s_new = f_col * s0_ref[bb, hh] + k_col * i_row
            snew_ref[bb, hh] = s_new
            o_scr[bb:bb + 1, ks] = jnp.sum(q_col * s_new, axis=0, keepdims=True)
    for hh in range(HEADS):
        ks = slice(hh * HEAD_DIM, (hh + 1) * HEAD_DIM)
        o = o_scr[:, ks]
        on = o * lax.rsqrt(jnp.mean(o * o, axis=-1, keepdims=True) + EPS)
        ob_ref[:, ks] = (on * gout_ref[:, ks] * gs_ref[:, ks].astype(F32)).astype(BF16)


def _hgrn_decode(lft, kkt, qst, ib, gs, gout, s0_all, s_new_all, *, layer, tb):
    m = ib.shape[0]
    col_spec = pl.BlockSpec((1, SEG, tb), lambda i: (i, 0, 0))
    row_spec = pl.BlockSpec((tb, SEG), lambda i: (i, 0))
    state_spec = pl.BlockSpec((None, tb, HEADS, HEAD_DIM, HEAD_DIM), lambda i: (layer, i, 0, 0, 0))
    in_specs = [col_spec, col_spec, col_spec, row_spec, row_spec,
                _layer_vec_spec(layer, SEG, 1), state_spec]
    args = [lft, kkt, qst, ib, gs, gout, s0_all]
    aliases = {}
    if s_new_all is not None:
        in_specs.append(pl.BlockSpec(memory_space=pl.ANY))
        args.append(s_new_all)
        aliases = {len(args) - 1: 1}
    return pl.pallas_call(
        _hgrn_decode_kernel,
        grid=(m // tb,),
        in_specs=in_specs,
        out_specs=[row_spec, state_spec],
        out_shape=[jax.ShapeDtypeStruct((m, SEG), BF16),
                   jax.ShapeDtypeStruct(s0_all.shape, F32)],
        input_output_aliases=aliases,
        scratch_shapes=[pltpu.VMEM((tb, SEG), F32)],
        compiler_params=pltpu.CompilerParams(
            dimension_semantics=("arbitrary",), vmem_limit_bytes=VMEM_LIMIT),
        name="hgrn_decode",
    )(*args)


def _outproj_kernel(oa_ref, ob_ref, w_ref, x_ref, g_ref, o_ref):
    mix = _dot(oa_ref[...], w_ref[:SEG, :]) + _dot(ob_ref[...], w_ref[SEG:, :])
    o_ref[...] = x_ref[...] + _rms_norm(mix, g_ref[...])


def _outproj(oa, ob, w, x, g, *, layer, tm):
    m, d = x.shape
    return pl.pallas_call(
        _outproj_kernel,
        grid=(m // tm,),
        in_specs=[pl.BlockSpec((tm, SEG), lambda i: (i, 0)),
                  pl.BlockSpec((tm, SEG), lambda i: (i, 0)),
                  pl.BlockSpec((None,) + w.shape[1:], lambda i: (layer, 0, 0)),
                  pl.BlockSpec((tm, d), lambda i: (i, 0)),
                  _layer_vec_spec(layer, d, 1)],
        out_specs=pl.BlockSpec((tm, d), lambda i: (i, 0)),
        out_shape=jax.ShapeDtypeStruct((m, d), F32),
        compiler_params=pltpu.CompilerParams(
            dimension_semantics=("arbitrary",), vmem_limit_bytes=VMEM_LIMIT),
        name="outproj",
    )(oa, ob, w, x, g)


def _ffn_kernel(x_ref, gpre_ref, wg_ref, wu_ref, wd_ref, gpost_ref, o_ref, h_scr, acc_scr):
    f = pl.program_id(1)

    @pl.when(f == 0)
    def _():
        h_scr[...] = _rms_norm(x_ref[...], gpre_ref[...]).astype(BF16)
        acc_scr[...] = jnp.zeros_like(acc_scr)

    h = h_scr[...]
    gate = _dot(h, wg_ref[...])
    up = _dot(h, wu_ref[...])
    t = (gate * _sigmoid(gate) * up).astype(BF16)
    acc_scr[...] += _dot(t, wd_ref[...])

    @pl.when(f == pl.num_programs(1) - 1)
    def _():
        o_ref[...] = x_ref[...] + _rms_norm(acc_scr[...], gpost_ref[...])


def _ffn(x, gpre, wg, wu, wd, gpost, *, layer, tm, tf):
    m, d = x.shape
    dff = wg.shape[2]
    return pl.pallas_call(
        _ffn_kernel,
        grid=(m // tm, dff // tf),
        in_specs=[pl.BlockSpec((tm, d), lambda i, f: (i, 0)),
                  _layer_vec_spec(layer, d, 2),
                  pl.BlockSpec((None, d, tf), lambda i, f: (layer, 0, f)),
                  pl.BlockSpec((None, d, tf), lambda i, f: (layer, 0, f)),
                  pl.BlockSpec((None, tf, d), lambda i, f: (layer, f, 0)),
                  _layer_vec_spec(layer, d, 2)],
        out_specs=pl.BlockSpec((tm, d), lambda i, f: (i, 0)),
        out_shape=jax.ShapeDtypeStruct((m, d), F32),
        scratch_shapes=[pltpu.VMEM((tm, d), BF16), pltpu.VMEM((tm, d), F32)],
        compiler_params=pltpu.CompilerParams(
            dimension_semantics=("arbitrary", "arbitrary"), vmem_limit_bytes=VMEM_LIMIT),
        name="ffn",
    )(x, gpre, wg, wu, wd, gpost)


PROMPT_TM = 512
FFN_TF = 512
DECODE_TB = 8


def kernel(x_prompt, x_sample, state_hgrn, norm_mix_pre, norm_mix_post, w_in, ln_v_gain, ln_v_bias,
           spatial_w, spatial_b, lb_param, hgrn_out_gain, w_out, norm_ffn_pre, norm_ffn_post,
           w_gate, w_up, w_down):
    depth = w_in.shape[0]
    batch, seq_len, d = x_prompt.shape
    dec_batch, dec_seq, _ = x_sample.shape
    assert dec_seq == 1 and seq_len % PROMPT_TM == 0 and PROMPT_TM % CHUNK == 0
    assert dec_batch % DECODE_TB == 0 and w_in.shape[2] == 6 * SEG and w_out.shape[1] == 2 * SEG

    w_in_b = w_in.astype(BF16)
    w_out_b = w_out.astype(BF16)
    w_gate_b = w_gate.astype(BF16)
    w_up_b = w_up.astype(BF16)
    w_down_b = w_down.astype(BF16)
    lev = jnp.asarray(_level_table())
    vec = lambda a: a.reshape(depth, 1, -1)
    g_mix_pre, g_mix_post = vec(norm_mix_pre), vec(norm_mix_post)
    g_ffn_pre, g_ffn_post = vec(norm_ffn_pre), vec(norm_ffn_post)
    ln_g, ln_b, g_out = vec(ln_v_gain), vec(ln_v_bias), vec(hgrn_out_gain)
    spatial_bt = spatial_b.transpose(0, 2, 1)
    w00 = vec(jnp.repeat(spatial_w[:, :, 0, 0], HEAD_DIM, axis=1))
    b0 = vec(jnp.repeat(spatial_b[:, :, 0], HEAD_DIM, axis=1))

    xp = x_prompt.reshape(batch * seq_len, d)
    xs = x_sample.reshape(dec_batch, d)
    s0t_zero = jnp.zeros((batch, HEADS, HEAD_DIM, HEAD_DIM), F32)
    sp_out, vp_out, vs_out = [], [], []
    ss_all = None

    for l in range(depth):
        oa, vrows, qs, lf, kk, ib, gs = _inproj(
            xp, g_mix_pre, w_in_b, ln_g, ln_b, spatial_w, spatial_bt, lb_param,
            layer=l, decode=False, tm=PROMPT_TM, seq_len=seq_len)
        ob, s_fin = _hgrn_prompt(qs, lf, kk, ib, gs, g_out, s0t_zero, lev,
                                 layer=l, batch=batch, seq_len=seq_len)
        xp = _outproj(oa, ob, w_out_b, xp, g_mix_post, layer=l, tm=PROMPT_TM)
        xp = _ffn(xp, g_ffn_pre, w_gate_b, w_up_b, w_down_b, g_ffn_post,
                  layer=l, tm=PROMPT_TM, tf=FFN_TF)
        sp_out.append(s_fin)
        vp_out.append(vrows.reshape(batch, CHUNK, SEG))
        oa, vrows, qs, lf, kk, ib, gs = _inproj(
            xs, g_mix_pre, w_in_b, ln_g, ln_b, w00, b0, lb_param,
            layer=l, decode=True, tm=dec_batch, seq_len=1)
        cols = lambda a: a.reshape(dec_batch // DECODE_TB, DECODE_TB, SEG).transpose(0, 2, 1)
        ob, ss_all = _hgrn_decode(cols(lf), cols(kk), cols(qs), ib, gs, g_out, state_hgrn, ss_all,
                                  layer=l, tb=DECODE_TB)
        xs = _outproj(oa, ob, w_out_b, xs, g_mix_post, layer=l, tm=dec_batch)
        xs = _ffn(xs, g_ffn_pre, w_gate_b, w_up_b, w_down_b, g_ffn_post,
                  layer=l, tm=dec_batch, tf=FFN_TF)
        vs_out.append(vrows.reshape(dec_batch, 1, SEG))

    return (xp.reshape(batch, seq_len, d), xs.reshape(dec_batch, 1, d),
            jnp.stack(sp_out), ss_all, jnp.stack(vp_out), jnp.stack(vs_out))
```

```python
import functools

import numpy as np
import jax
import jax.numpy as jnp
from jax import lax
from jax.experimental import pallas as pl
from jax.experimental.pallas import tpu as pltpu

F32 = jnp.float32
BF16 = jnp.bfloat16

EPS = 1e-6
LANES = 128
SUBLANES = 8
SEG = 1024
HEADS = 8
HEAD_DIM = SEG // HEADS
CHUNK = 128
N_LEVELS = 7
ROW_CHUNK = 256
LOG2E = np.float32(1.4426950408889634)
VMEM_LIMIT = 56 * 1024 * 1024


def _rms_norm(x, g):
    return x * lax.rsqrt(jnp.mean(x * x, axis=-1, keepdims=True) + EPS) * g


def _sigmoid(x):
    return 1.0 / (1.0 + jnp.exp(-x))


def _gelu(x):
    return 0.5 * x * (1.0 + lax.erf(x * np.float32(np.sqrt(0.5))))


def _dot(a, b):
    return jnp.dot(a, b, preferred_element_type=F32)


def _dot_nt(a, b):
    return lax.dot_general(a, b, (((1,), (1,)), ((), ())), preferred_element_type=F32)


def _dot_tn(a, b):
    return lax.dot_general(a, b, (((0,), (0,)), ((), ())), preferred_element_type=F32)


def _forget_lower_bound(lbp, layer):
    m = jnp.max(lbp, axis=0, keepdims=True)
    e = jnp.exp(lbp - m)
    sm = e / jnp.sum(e, axis=0, keepdims=True)
    cum = sm[0:1]
    for r in range(1, layer + 1):
        cum = cum + sm[r:r + 1]
    return cum - sm[0:1]


def _layer_vec_spec(layer, width, grid_rank):
    if grid_rank == 1:
        return pl.BlockSpec((None, 1, width), lambda i: (layer, 0, 0))
    return pl.BlockSpec((None, 1, width), lambda i, j: (layer, 0, 0))


def _inproj_kernel(x_ref, gpre_ref, w_ref, lng_ref, lnb_ref, ws_ref, bs_ref, lbp_ref,
                   oa_ref, vrows_ref, qs_ref, lf_ref, kk_ref, ib_ref, gs_ref,
                   *rest, layer, decode):
    wb_ref = rest[0] if decode else None
    h_scr, u_scr = rest[-2:]
    j = pl.program_id(1)
    tm = x_ref.shape[0]
    rc = min(tm, ROW_CHUNK)
    if decode:
        wb_ref[...] = w_ref[...].astype(BF16)
    wmat_ref = wb_ref if decode else w_ref

    def segment(epilogue, prologue=None):
        for r in range(tm // rc):
            rows = slice(r * rc, (r + 1) * rc)
            if prologue is not None:
                prologue(rows)
            epilogue(_dot(h_scr[rows, :], wmat_ref[...]), rows, r)

    @pl.when(j == 0)
    def _():
        def norm_rows(rows):
            h_scr[rows, :] = _rms_norm(x_ref[rows, :], gpre_ref[...]).astype(BF16)

        def epilogue(acc, rows, r):
            u_scr[rows, :] = _gelu(acc)
        segment(epilogue, norm_rows)

    @pl.when(j == 1)
    def _():
        if not decode:
            row = lax.broadcasted_iota(jnp.int32, (CHUNK, CHUNK), 0)
            col = lax.broadcasted_iota(jnp.int32, (CHUNK, CHUNK), 1)
            causal = row >= col

        def epilogue(acc, rows, r):
            gv = _gelu(acc)
            mu = jnp.mean(gv, axis=-1, keepdims=True)
            vc = gv - mu
            v = vc * lax.rsqrt(jnp.mean(vc * vc, axis=-1, keepdims=True) + EPS)
            v = v * lng_ref[...] + lnb_ref[...]
            if decode:
                vrows_ref[rows, :] = v
                oa_ref[rows, :] = (u_scr[rows, :] * (v * ws_ref[...] + bs_ref[...])).astype(BF16)
                return
            if r == tm // rc - 1:
                vrows_ref[...] = v[rc - CHUNK:, :]
            vb = v.astype(BF16)
            for hh in range(HEADS):
                cs = slice(hh * HEAD_DIM, (hh + 1) * HEAD_DIM)
                w_h = jnp.where(causal, ws_ref[hh], 0.0).astype(BF16)
                b_h = bs_ref[:, hh:hh + 1]
                for c in range(rc // CHUNK):
                    s = _dot(w_h, vb[c * CHUNK:(c + 1) * CHUNK, cs]) + b_h
                    rs = slice(rows.start + c * CHUNK, rows.start + (c + 1) * CHUNK)
                    oa_ref[rs, cs] = (u_scr[rs, cs] * s).astype(BF16)
        segment(epilogue)

    @pl.when(j == 2)
    def _():
        def epilogue(acc, rows, r):
            qs_ref[rows, :] = (acc * _sigmoid(acc)).astype(BF16)
        segment(epilogue)

    @pl.when(j == 3)
    def _():
        lb = _forget_lower_bound(lbp_ref[...], layer)
        log_lb = jnp.log(lb)
        log_1m = jnp.log1p(-lb)

        def epilogue(z, rows, r):
            ez = jnp.exp(-jnp.abs(z))
            one_plus = 1.0 + ez
            logsig = jnp.minimum(z, 0.0) - jnp.log(one_plus)
            sig_neg = jnp.where(z >= 0.0, ez, 1.0) / one_plus
            if layer == 0:
                lf_ref[rows, :] = logsig
                kk_ref[rows, :] = sig_neg.astype(BF16)
            else:
                b2 = log_1m + logsig
                amax = jnp.maximum(log_lb, b2)
                lf_ref[rows, :] = amax + jnp.log(1.0 + jnp.exp(-jnp.abs(log_lb - b2)))
                kk_ref[rows, :] = ((1.0 - lb) * sig_neg).astype(BF16)
        segment(epilogue)

    @pl.when(j == 4)
    def _():
        def epilogue(acc, rows, r):
            ib_ref[rows, :] = acc.astype(BF16)
        segment(epilogue)

    @pl.when(j == 5)
    def _():
        def epilogue(acc, rows, r):
            gs_ref[rows, :] = (acc * _sigmoid(acc)).astype(BF16)
        segment(epilogue)


def _inproj(x, gpre, w, lng, lnb, ws, bs, lbp, *, layer, decode, tm, seq_len):
    m, d = x.shape
    n_seg = w.shape[-1] // SEG
    if decode:
        assert m == tm <= ROW_CHUNK
        vrows_shape, vrows_spec = (m, SEG), pl.BlockSpec((tm, SEG), lambda i, j: (i, 0))
        ws_spec = _layer_vec_spec(layer, SEG, 2)
        bs_spec = _layer_vec_spec(layer, SEG, 2)
        w_spec = pl.BlockSpec((None, d, SEG), lambda i, j: (layer, 0, j))
    else:
        w_spec = pl.BlockSpec((d, SEG), lambda i, j: (0, j))
        tiles_per_seq = seq_len // tm
        vrows_shape = (m // seq_len * CHUNK, SEG)
        vrows_spec = pl.BlockSpec((CHUNK, SEG), lambda i, j: (i // tiles_per_seq, 0))
        ws_spec = pl.BlockSpec((None, HEADS, CHUNK, CHUNK), lambda i, j: (layer, 0, 0, 0))
        bs_spec = pl.BlockSpec((None, CHUNK, HEADS), lambda i, j: (layer, 0, 0))
    row_spec = pl.BlockSpec((tm, SEG), lambda i, j: (i, 0))
    out_shape = [
        jax.ShapeDtypeStruct((m, SEG), BF16),
        jax.ShapeDtypeStruct(vrows_shape, F32),
        jax.ShapeDtypeStruct((m, SEG), BF16),
        jax.ShapeDtypeStruct((m, SEG), F32),
        jax.ShapeDtypeStruct((m, SEG), BF16),
        jax.ShapeDtypeStruct((m, SEG), BF16),
        jax.ShapeDtypeStruct((m, SEG), BF16),
    ]
    out_specs = [row_spec, vrows_spec, row_spec, row_spec, row_spec, row_spec, row_spec]
    if decode:
        out_shape.append(jax.ShapeDtypeStruct((d, n_seg * SEG), BF16))
        out_specs.append(pl.BlockSpec((d, SEG), lambda i, j: (0, j)))
    return pl.pallas_call(
        functools.partial(_inproj_kernel, layer=layer, decode=decode),
        grid=(m // tm, n_seg),
        in_specs=[
            pl.BlockSpec((tm, d), lambda i, j: (i, 0)),
            _layer_vec_spec(layer, d, 2),
            w_spec,
            _layer_vec_spec(layer, SEG, 2), _layer_vec_spec(layer, SEG, 2), ws_spec, bs_spec,
            pl.BlockSpec(lbp.shape, lambda i, j: (0, 0)),
        ],
        out_specs=out_specs,
        out_shape=out_shape,
        scratch_shapes=[pltpu.VMEM((tm, d), BF16), pltpu.VMEM((tm, SEG), F32)],
        compiler_params=pltpu.CompilerParams(
            dimension_semantics=("arbitrary", "arbitrary"), vmem_limit_bytes=VMEM_LIMIT),
        name="inproj_decode" if decode else "inproj_prompt",
    )(x, gpre, w, lng, lnb, ws, bs, lbp)


def _level_table():
    t = np.arange(CHUNK)[:, None]
    s = np.arange(CHUNK)[None, :]
    x = t ^ s
    lev = np.floor(np.log2(np.maximum(x, 1))).astype(np.int32)
    lev = np.where(t == s, -1, lev)
    lev = np.where(t < s, -2, lev)
    return lev.astype(np.int32)


def _log_gap_to_block_boundary(b, level):
    half = 1 << level
    blk = 2 * half
    k = b.shape[1]
    if half >= SUBLANES:
        bb = b.reshape(CHUNK // blk, blk, k)
        r = bb[:, half - 1:half, :]
        gap = jnp.concatenate([r - bb[:, :half, :], bb[:, half:, :] - r], axis=1)
        return gap.reshape(CHUNK, k)
    b3 = b.reshape(CHUNK // SUBLANES, SUBLANES, k)
    sub = lax.broadcasted_iota(jnp.int32, (1, SUBLANES, k), 1)
    r = None
    for start in range(0, SUBLANES, blk):
        cand = jnp.broadcast_to(b3[:, start + half - 1:start + half, :], b3.shape)
        r = cand if r is None else jnp.where(sub >= start, cand, r)
    sign = jnp.where((sub & half) != 0, 1.0, -1.0)
    return ((b3 - r) * sign).reshape(CHUNK, k)


def _second_half_rows_from(a, b, level):
    half = 1 << level
    blk = 2 * half
    k = a.shape[1]
    if half >= SUBLANES:
        a3 = a.reshape(CHUNK // blk, blk, k)
        b3 = b.reshape(CHUNK // blk, blk, k)
        return jnp.concatenate([b3[:, :half, :], a3[:, half:, :]], axis=1).reshape(CHUNK, k)
    sub = lax.broadcasted_iota(jnp.int32, (1, SUBLANES, k), 1)
    a3 = a.reshape(CHUNK // SUBLANES, SUBLANES, k)
    b3 = b.reshape(CHUNK // SUBLANES, SUBLANES, k)
    return jnp.where((sub & half) != 0, a3, b3).reshape(CHUNK, k)


def _hgrn_prompt_kernel(qs_ref, lf_ref, kk_ref, ib_ref, gs_ref, gout_ref, s0t_ref, lev_ref,
                        ob_ref, sfin_ref, st_scr):
    c = pl.program_id(1)

    @pl.when(c == 0)
    def _():
        st_scr[...] = s0t_ref[0]

    lev = lev_ref[...]
    row = lax.broadcasted_iota(jnp.int32, (CHUNK, CHUNK), 0)
    col = lax.broadcasted_iota(jnp.int32, (CHUNK, CHUNK), 1)
    tri = jnp.where(row >= col, 1.0, 0.0).astype(BF16)

    odd_row = (lax.broadcasted_iota(jnp.int32, (CHUNK, HEAD_DIM), 0) & 1) == 1
    for hh in range(HEADS):
        cs = slice(hh * HEAD_DIM, (hh + 1) * HEAD_DIM)
        lf2 = lf_ref[:, cs] * LOG2E
        qb = qs_ref[:, cs]
        kb = kk_ref[:, cs]
        q = qb.astype(F32)
        kk = kb.astype(F32)
        v = ib_ref[:, cs]
        st = st_scr[hh]

        hi = lf2.astype(BF16)
        r1 = lf2 - hi.astype(F32)
        mid = r1.astype(BF16)
        lo = (r1 - mid.astype(F32)).astype(BF16)
        b = _dot(tri, hi) + _dot(tri, mid) + _dot(tri, lo)
        b_last = b[CHUNK - 1:CHUNK, :]

        o = _dot_nt((q * jnp.exp2(b)).astype(BF16), st.astype(BF16))

        a = jnp.where(lev == -1, _dot_nt(qb, kb), 0.0)
        for level in range(N_LEVELS):
            if level == 0:
                d = jnp.where(odd_row, lf2, 0.0)
            else:
                d = _log_gap_to_block_boundary(b, level)
            x = (_second_half_rows_from(q, kk, level) * jnp.exp2(d)).astype(BF16)
            a = jnp.where(lev == level, _dot_nt(x, x), a)
        o = o + _dot(a.astype(BF16), v)

        kd = (kk * jnp.exp2(b_last - b)).astype(BF16)
        st_scr[hh] = st * jnp.exp2(b_last) + _dot_tn(v, kd)

        on = o * lax.rsqrt(jnp.mean(o * o, axis=-1, keepdims=True) + EPS)
        ob_ref[:, cs] = (on * gout_ref[:, cs] * gs_ref[:, cs].astype(F32)).astype(BF16)

    @pl.when(c == pl.num_programs(1) - 1)
    def _():
        for hh in range(HEADS):
            sfin_ref[0, hh] = st_scr[hh].T


def _hgrn_prompt(qs, lf, kk, ib, gs, gout, s0t, lev, *, layer, batch, seq_len):
    m = qs.shape[0]
    nc = seq_len // CHUNK
    row_spec = pl.BlockSpec((CHUNK, SEG), lambda b, c: (b * nc + c, 0))
    state_spec = pl.BlockSpec((1, HEADS, HEAD_DIM, HEAD_DIM), lambda b, c: (b, 0, 0, 0))
    return pl.pallas_call(
        _hgrn_prompt_kernel,
        grid=(batch, nc),
        in_specs=[row_spec, row_spec, row_spec, row_spec, row_spec,
                  _layer_vec_spec(layer, SEG, 2),
                  state_spec,
                  pl.BlockSpec((CHUNK, CHUNK), lambda b, c: (0, 0))],
        out_specs=[row_spec, state_spec],
        out_shape=[jax.ShapeDtypeStruct((m, SEG), BF16),
                   jax.ShapeDtypeStruct((batch, HEADS, HEAD_DIM, HEAD_DIM), F32)],
        scratch_shapes=[pltpu.VMEM((HEADS, HEAD_DIM, HEAD_DIM), F32)],
        compiler_params=pltpu.CompilerParams(
            dimension_semantics=("arbitrary", "arbitrary"), vmem_limit_bytes=VMEM_LIMIT),
        name="hgrn_prompt",
    )(qs, lf, kk, ib, gs, gout, s0t, lev)


def _hgrn_decode_kernel(lft_ref, kkt_ref, qst_ref, ib_ref, gs_ref, gout_ref, s0_ref, *rest):
    ob_ref, snew_ref, o_scr = rest[-3:]
    tb = s0_ref.shape[0]
    for bb in range(tb):
        for hh in range(HEADS):
            ks = slice(hh * HEAD_DIM, (hh + 1) * HEAD_DIM)
            f_col = jnp.exp(lft_ref[0, ks, bb:bb + 1])
            k_col = kkt_ref[0, ks, bb:bb + 1].astype(F32)
            q_col = qst_ref[0, ks, bb:bb + 1].astype(F32)
            i_row = ib_ref[bb:bb + 1, ks].astype(F32)
            s_new = f_col * s0_ref[bb, hh] + k_col * i_row
            snew_ref[bb, hh] = s_new
            o_scr[bb:bb + 1, ks] = jnp.sum(q_col * s_new, axis=0, keepdims=True)
    for hh in range(HEADS):
        ks = slice(hh * HEAD_DIM, (hh + 1) * HEAD_DIM)
        o = o_scr[:, ks]
        on = o * lax.rsqrt(jnp.mean(o * o, axis=-1, keepdims=True) + EPS)
        ob_ref[:, ks] = (on * gout_ref[:, ks] * gs_ref[:, ks].astype(F32)).astype(BF16)


def _hgrn_decode(lft, kkt, qst, ib, gs, gout, s0_all, s_new_all, *, layer, tb):
    m = ib.shape[0]
    col_spec = pl.BlockSpec((1, SEG, tb), lambda i: (i, 0, 0))
    row_spec = pl.BlockSpec((tb, SEG), lambda i: (i, 0))
    state_spec = pl.BlockSpec((None, tb, HEADS, HEAD_DIM, HEAD_DIM), lambda i: (layer, i, 0, 0, 0))
    in_specs = [col_spec, col_spec, col_spec, row_spec, row_spec,
                _layer_vec_spec(layer, SEG, 1), state_spec]
    args = [lft, kkt, qst, ib, gs, gout, s0_all]
    aliases = {}
    if s_new_all is not None:
        in_specs.append(pl.BlockSpec(memory_space=pl.ANY))
        args.append(s_new_all)
        aliases = {len(args) - 1: 1}
    return pl.pallas_call(
        _hgrn_decode_kernel,
        grid=(m // tb,),
        in_specs=in_specs,
        out_specs=[row_spec, state_spec],
        out_shape=[jax.ShapeDtypeStruct((m, SEG), BF16),
                   jax.ShapeDtypeStruct(s0_all.shape, F32)],
        input_output_aliases=aliases,
        scratch_shapes=[pltpu.VMEM((tb, SEG), F32)],
        compiler_params=pltpu.CompilerParams(
            dimension_semantics=("arbitrary",), vmem_limit_bytes=VMEM_LIMIT),
        name="hgrn_decode",
    )(*args)


def _outproj_kernel(oa_ref, ob_ref, w_ref, x_ref, g_ref, gffn_ref, o_ref, h_ref):
    tm = x_ref.shape[0]
    rc = min(tm, ROW_CHUNK)
    for r in range(tm // rc):
        rows = slice(r * rc, (r + 1) * rc)
        mix = _dot(oa_ref[rows, :], w_ref[:SEG, :]) + _dot(ob_ref[rows, :], w_ref[SEG:, :])
        x_mid = x_ref[rows, :] + _rms_norm(mix, g_ref[...])
        o_ref[rows, :] = x_mid
        h_ref[rows, :] = _rms_norm(x_mid, gffn_ref[...]).astype(BF16)


def _outproj_cast_kernel(oa_ref, ob_ref, w_ref, x_ref, g_ref, gffn_ref, o_ref, h_ref, wb_ref,
                         acc_scr):
    k = pl.program_id(0)
    wb = w_ref[...].astype(BF16)
    wb_ref[...] = wb

    @pl.when(k == 0)
    def _():
        acc_scr[...] = _dot(oa_ref[...], wb)

    @pl.when(k == 1)
    def _():
        x_mid = x_ref[...] + _rms_norm(acc_scr[...] + _dot(ob_ref[...], wb), g_ref[...])
        o_ref[...] = x_mid
        h_ref[...] = _rms_norm(x_mid, gffn_ref[...]).astype(BF16)


def _outproj_cast(oa, ob, w, x, g, gffn, *, layer):
    m, d = x.shape
    whole = lambda shape: pl.BlockSpec(shape, lambda k: (0,) * len(shape))
    return pl.pallas_call(
        _outproj_cast_kernel,
        grid=(2,),
        in_specs=[whole((m, SEG)), whole((m, SEG)),
                  pl.BlockSpec((None, SEG, d), lambda k: (layer, k, 0)),
                  whole((m, d)),
                  _layer_vec_spec(layer, d, 1),
                  _layer_vec_spec(layer, d, 1)],
        out_specs=[whole((m, d)), whole((m, d)), pl.BlockSpec((SEG, d), lambda k: (k, 0))],
        out_shape=[jax.ShapeDtypeStruct((m, d), F32),
                   jax.ShapeDtypeStruct((m, d), BF16),
                   jax.ShapeDtypeStruct((2 * SEG, d), BF16)],
        scratch_shapes=[pltpu.VMEM((m, d), F32)],
        compiler_params=pltpu.CompilerParams(
            dimension_semantics=("arbitrary",), vmem_limit_bytes=VMEM_LIMIT),
        name="outproj_decode",
    )(oa, ob, w, x, g, gffn)


def _outproj(oa, ob, w, x, g, gffn, *, layer, tm):
    m, d = x.shape
    return pl.pallas_call(
        _outproj_kernel,
        grid=(m // tm,),
        in_specs=[pl.BlockSpec((tm, SEG), lambda i: (i, 0)),
                  pl.BlockSpec((tm, SEG), lambda i: (i, 0)),
                  pl.BlockSpec(w.shape, lambda i: (0, 0)),
                  pl.BlockSpec((tm, d), lambda i: (i, 0)),
                  _layer_vec_spec(layer, d, 1),
                  _layer_vec_spec(layer, d, 1)],
        out_specs=[pl.BlockSpec((tm, d), lambda i: (i, 0)),
                   pl.BlockSpec((tm, d), lambda i: (i, 0))],
        out_shape=[jax.ShapeDtypeStruct((m, d), F32),
                   jax.ShapeDtypeStruct((m, d), BF16)],
        compiler_params=pltpu.CompilerParams(
            dimension_semantics=("arbitrary",), vmem_limit_bytes=VMEM_LIMIT),
        name="outproj",
    )(oa, ob, w, x, g, gffn)


def _ffn_kernel(x_ref, h_ref, wg_ref, wu_ref, wd_ref, gpost_ref, o_ref, *rest, cast):
    acc_scr = rest[-1]
    f = pl.program_id(1)
    nf = pl.num_programs(1)
    tm = x_ref.shape[0]
    if cast:
        for src, dst in zip((wg_ref, wu_ref, wd_ref), rest[:3]):
            dst[...] = src[...].astype(BF16)
        wg_ref, wu_ref, wd_ref = rest[:3]

    def contribution(rows):
        h = h_ref[rows, :]
        gate = _dot(h, wg_ref[...])
        up = _dot(h, wu_ref[...])
        t = (gate * _sigmoid(gate) * up).astype(BF16)
        return _dot(t, wd_ref[...])

    @pl.when(f == 0)
    def _():
        acc_scr[...] = contribution(slice(None))

    @pl.when(jnp.logical_and(f > 0, f < nf - 1))
    def _():
        acc_scr[...] += contribution(slice(None))

    @pl.when(f == nf - 1)
    def _():
        rc = min(tm, ROW_CHUNK)
        for r in range(tm // rc):
            rows = slice(r * rc, (r + 1) * rc)
            acc = acc_scr[rows, :] + contribution(rows)
            o_ref[rows, :] = x_ref[rows, :] + _rms_norm(acc, gpost_ref[...])


def _ffn(x, h, wg, wu, wd, gpost, *, layer, tm, tf, cast):
    m, d = x.shape
    dff = wg.shape[-1]
    assert dff // tf >= 2
    col_spec = pl.BlockSpec((d, tf), lambda i, f: (0, f))
    row_spec = pl.BlockSpec((tf, d), lambda i, f: (f, 0))
    out_specs = [pl.BlockSpec((tm, d), lambda i, f: (i, 0))]
    out_shape = [jax.ShapeDtypeStruct((m, d), F32)]
    if cast:
        assert m == tm
        w_specs = [pl.BlockSpec((None, d, tf), lambda i, f: (layer, 0, f)),
                   pl.BlockSpec((None, d, tf), lambda i, f: (layer, 0, f)),
                   pl.BlockSpec((None, tf, d), lambda i, f: (layer, f, 0))]
        out_specs += [col_spec, col_spec, row_spec]
        out_shape += [jax.ShapeDtypeStruct((d, dff), BF16), jax.ShapeDtypeStruct((d, dff), BF16),
                      jax.ShapeDtypeStruct((dff, d), BF16)]
    else:
        w_specs = [col_spec, col_spec, row_spec]
    res = pl.pallas_call(
        functools.partial(_ffn_kernel, cast=cast),
        grid=(m // tm, dff // tf),
        in_specs=[pl.BlockSpec((tm, d), lambda i, f: (i, 0)),
                  pl.BlockSpec((tm, d), lambda i, f: (i, 0)),
                  *w_specs,
                  _layer_vec_spec(layer, d, 2)],
        out_specs=out_specs,
        out_shape=out_shape,
        scratch_shapes=[pltpu.VMEM((tm, d), F32)],
        compiler_params=pltpu.CompilerParams(
            dimension_semantics=("arbitrary", "arbitrary"), vmem_limit_bytes=VMEM_LIMIT),
        name="ffn_decode" if cast else "ffn",
    )(x, h, wg, wu, wd, gpost)
    return res if cast else res[0]


PROMPT_TM = 512
FFN_TF = 512
DECODE_TB = 8


def kernel(x_prompt, x_sample, state_hgrn, norm_mix_pre, norm_mix_post, w_in, ln_v_gain, ln_v_bias,
           spatial_w, spatial_b, lb_param, hgrn_out_gain, w_out, norm_ffn_pre, norm_ffn_post,
           w_gate, w_up, w_down):
    depth = w_in.shape[0]
    batch, seq_len, d = x_prompt.shape
    dec_batch, dec_seq, _ = x_sample.shape
    assert dec_seq == 1 and seq_len % PROMPT_TM == 0 and PROMPT_TM % CHUNK == 0
    assert dec_batch % DECODE_TB == 0 and w_in.shape[2] == 6 * SEG and w_out.shape[1] == 2 * SEG

    lev = jnp.asarray(_level_table())
    vec = lambda a: a.reshape(depth, 1, -1)
    g_mix_pre, g_mix_post = vec(norm_mix_pre), vec(norm_mix_post)
    g_ffn_pre, g_ffn_post = vec(norm_ffn_pre), vec(norm_ffn_post)
    ln_g, ln_b, g_out = vec(ln_v_gain), vec(ln_v_bias), vec(hgrn_out_gain)
    spatial_bt = spatial_b.transpose(0, 2, 1)
    w00 = vec(jnp.repeat(spatial_w[:, :, 0, 0], HEAD_DIM, axis=1))
    b0 = vec(jnp.repeat(spatial_b[:, :, 0], HEAD_DIM, axis=1))

    xp = x_prompt.reshape(batch * seq_len, d)
    xs = x_sample.reshape(dec_batch, d)
    s0t_zero = jnp.zeros((batch, HEADS, HEAD_DIM, HEAD_DIM), F32)
    sp_out, vp_out, vs_out = [], [], []
    ss_all = None

    for l in range(depth):
        oa, vrows, qs, lf, kk, ib, gs, w_in_b = _inproj(
            xs, g_mix_pre, w_in, ln_g, ln_b, w00, b0, lb_param,
            layer=l, decode=True, tm=dec_batch, seq_len=1)
        cols = lambda a: a.reshape(dec_batch // DECODE_TB, DECODE_TB, SEG).transpose(0, 2, 1)
        ob, ss_all = _hgrn_decode(cols(lf), cols(kk), cols(qs), ib, gs, g_out, state_hgrn, ss_all,
                                  layer=l, tb=DECODE_TB)
        xs, hs, w_out_b = _outproj_cast(oa, ob, w_out, xs, g_mix_post, g_ffn_pre, layer=l)
        xs, w_gate_b, w_up_b, w_down_b = _ffn(xs, hs, w_gate, w_up, w_down, g_ffn_post,
                                              layer=l, tm=dec_batch, tf=FFN_TF, cast=True)
        vs_out.append(vrows.reshape(dec_batch, 1, SEG))
        oa, vrows, qs, lf, kk, ib, gs = _inproj(
            xp, g_mix_pre, w_in_b, ln_g, ln_b, spatial_w, spatial_bt, lb_param,
            layer=l, decode=False, tm=PROMPT_TM, seq_len=seq_len)
        ob, s_fin = _hgrn_prompt(qs, lf, kk, ib, gs, g_out, s0t_zero, lev,
                                 layer=l, batch=batch, seq_len=seq_len)
        xp, hp = _outproj(oa, ob, w_out_b, xp, g_mix_post, g_ffn_pre, layer=l, tm=PROMPT_TM)
        xp = _ffn(xp, hp, w_gate_b, w_up_b, w_down_b, g_ffn_post,
                  layer=l, tm=PROMPT_TM, tf=FFN_TF, cast=False)
        sp_out.append(s_fin)
        vp_out.append(vrows.reshape(batch, CHUNK, SEG))

    return (xp.reshape(batch, seq_len, d), xs.reshape(dec_batch, 1, d),
            jnp.stack(sp_out), ss_all, jnp.stack(vp_out), jnp.stack(vs_out))
```

```python
import functools

import numpy as np
import jax
import jax.numpy as jnp
from jax import lax
from jax.experimental import pallas as pl
from jax.experimental.pallas import tpu as pltpu

F32 = jnp.float32
BF16 = jnp.bfloat16

EPS = 1e-6
LANES = 128
SUBLANES = 8
SEG = 1024
HEADS = 8
HEAD_DIM = SEG // HEADS
CHUNK = 128
N_LEVELS = 7
ROW_CHUNK = 256
COL_CHUNK = 256
V_ROW_BLOCK = 512
FFN_LAST_ROW_BLOCK = 512
LOG2E = np.float32(1.4426950408889634)
VMEM_LIMIT = 60 * 1024 * 1024


def _rms_norm(x, g):
    return x * lax.rsqrt(jnp.mean(x * x, axis=-1, keepdims=True) + EPS) * g


def _sigmoid(x):
    return 1.0 / (1.0 + jnp.exp(-x))


def _gelu(x):
    return 0.5 * x * (1.0 + lax.erf(x * np.float32(np.sqrt(0.5))))


def _dot(a, b):
    return jnp.dot(a, b, preferred_element_type=F32)


def _dot_nt(a, b):
    return lax.dot_general(a, b, (((1,), (1,)), ((), ())), preferred_element_type=F32)


def _dot_tn(a, b):
    return lax.dot_general(a, b, (((0,), (0,)), ((), ())), preferred_element_type=F32)


def _forget_lower_bound(lbp, layer):
    m = jnp.max(lbp, axis=0, keepdims=True)
    e = jnp.exp(lbp - m)
    sm = e / jnp.sum(e, axis=0, keepdims=True)
    cum = sm[0:1]
    for r in range(1, layer + 1):
        cum = cum + sm[r:r + 1]
    return cum - sm[0:1]


def _layer_vec_spec(layer, width, grid_rank):
    if grid_rank == 1:
        return pl.BlockSpec((None, 1, width), lambda i: (layer, 0, 0))
    return pl.BlockSpec((None, 1, width), lambda i, j: (layer, 0, 0))


def _by_columns(h_ref, w_ref, epilogue):
    for c in range(SEG // COL_CHUNK):
        cols = slice(c * COL_CHUNK, (c + 1) * COL_CHUNK)
        epilogue(_dot(h_ref[...], w_ref[:, cols]), cols)


def _inproj_a_kernel(x_ref, gpre_ref, w_ref, lng_ref, lnb_ref, ws_ref, bs_ref,
                     oa_ref, vrows_ref, h_ref, *rest, decode):
    wb_ref = rest[0] if decode else None
    u_scr = rest[-1]
    j = pl.program_id(1)
    tm = x_ref.shape[0]
    if decode:
        wb_ref[...] = w_ref[...].astype(BF16)
    wmat_ref = wb_ref if decode else w_ref

    @pl.when(j == 0)
    def _():
        rc = min(tm, ROW_CHUNK)
        for r in range(tm // rc):
            rows = slice(r * rc, (r + 1) * rc)
            h_ref[rows, :] = _rms_norm(x_ref[rows, :], gpre_ref[...]).astype(BF16)

        def epilogue(acc, cols):
            u_scr[:, cols] = _gelu(acc)
        _by_columns(h_ref, wmat_ref, epilogue)

    @pl.when(j == 1)
    def _():
        if not decode:
            row = lax.broadcasted_iota(jnp.int32, (CHUNK, CHUNK), 0)
            col = lax.broadcasted_iota(jnp.int32, (CHUNK, CHUNK), 1)
            causal = row >= col
        rb = min(tm, V_ROW_BLOCK)
        for r in range(tm // rb):
            rows = slice(r * rb, (r + 1) * rb)
            gv = _gelu(_dot(h_ref[rows, :], wmat_ref[...]))
            mu = jnp.mean(gv, axis=-1, keepdims=True)
            vc = gv - mu
            v = vc * lax.rsqrt(jnp.mean(vc * vc, axis=-1, keepdims=True) + EPS)
            v = v * lng_ref[...] + lnb_ref[...]
            if decode:
                vrows_ref[rows, :] = v
                oa_ref[rows, :] = (u_scr[rows, :] * (v * ws_ref[...] + bs_ref[...])).astype(BF16)
                continue
            if r == tm // rb - 1:
                vrows_ref[...] = v[rb - CHUNK:, :]
            vb = v.astype(BF16)
            for hh in range(HEADS):
                cs = slice(hh * HEAD_DIM, (hh + 1) * HEAD_DIM)
                w_h = jnp.where(causal, ws_ref[hh], 0.0).astype(BF16)
                b_h = bs_ref[:, hh:hh + 1]
                for c in range(rb // CHUNK):
                    s = _dot(w_h, vb[c * CHUNK:(c + 1) * CHUNK, cs]) + b_h
                    rs = slice(rows.start + c * CHUNK, rows.start + (c + 1) * CHUNK)
                    oa_ref[rs, cs] = (u_scr[rs, cs] * s).astype(BF16)


def _inproj_a(x, gpre, w, lng, lnb, ws, bs, *, layer, decode, tm, seq_len):
    m, d = x.shape
    if decode:
        assert m == tm
        vrows_shape, vrows_spec = (m, SEG), pl.BlockSpec((tm, SEG), lambda i, j: (i, 0))
        ws_spec = _layer_vec_spec(layer, SEG, 2)
        bs_spec = _layer_vec_spec(layer, SEG, 2)
        w_spec = pl.BlockSpec((None, d, SEG), lambda i, j: (layer, 0, j))
    else:
        assert seq_len % tm == 0 and tm % CHUNK == 0
        tiles_per_seq = seq_len // tm
        vrows_shape = (m // seq_len * CHUNK, SEG)
        vrows_spec = pl.BlockSpec((CHUNK, SEG), lambda i, j: (i // tiles_per_seq, 0))
        ws_spec = pl.BlockSpec((None, HEADS, CHUNK, CHUNK), lambda i, j: (layer, 0, 0, 0))
        bs_spec = pl.BlockSpec((None, CHUNK, HEADS), lambda i, j: (layer, 0, 0))
        w_spec = pl.BlockSpec((d, SEG), lambda i, j: (0, j))
    out_shape = [jax.ShapeDtypeStruct((m, SEG), BF16),
                 jax.ShapeDtypeStruct(vrows_shape, F32),
                 jax.ShapeDtypeStruct((m, d), BF16)]
    out_specs = [pl.BlockSpec((tm, SEG), lambda i, j: (i, 0)), vrows_spec,
                 pl.BlockSpec((tm, d), lambda i, j: (i, 0))]
    if decode:
        out_shape.append(jax.ShapeDtypeStruct((d, 2 * SEG), BF16))
        out_specs.append(pl.BlockSpec((d, SEG), lambda i, j: (0, j)))
    return pl.pallas_call(
        functools.partial(_inproj_a_kernel, decode=decode),
        grid=(m // tm, 2),
        in_specs=[pl.BlockSpec((tm, d), lambda i, j: (i, 0)),
                  _layer_vec_spec(layer, d, 2),
                  w_spec,
                  _layer_vec_spec(layer, SEG, 2), _layer_vec_spec(layer, SEG, 2), ws_spec, bs_spec],
        out_specs=out_specs,
        out_shape=out_shape,
        scratch_shapes=[pltpu.VMEM((tm, SEG), F32)],
        compiler_params=pltpu.CompilerParams(
            dimension_semantics=("arbitrary", "arbitrary"), vmem_limit_bytes=VMEM_LIMIT),
        name="inproj_a_decode" if decode else "inproj_a",
    )(x, gpre, w, lng, lnb, ws, bs)


def _inproj_b_kernel(h_ref, w_ref, lbp_ref, qs_ref, lf_ref, kk_ref, ib_ref, gs_ref, *rest,
                     layer, decode):
    j = pl.program_id(1)
    if decode:
        wb_ref = rest[0]
        wb_ref[...] = w_ref[...].astype(BF16)
        w_ref = wb_ref

    @pl.when(j == 0)
    def _():
        def epilogue(acc, cols):
            qs_ref[:, cols] = (acc * _sigmoid(acc)).astype(BF16)
        _by_columns(h_ref, w_ref, epilogue)

    @pl.when(j == 1)
    def _():
        lb_all = _forget_lower_bound(lbp_ref[...], layer)

        def epilogue(z, cols):
            ez = jnp.exp(-jnp.abs(z))
            one_plus = 1.0 + ez
            logsig = jnp.minimum(z, 0.0) - jnp.log(one_plus)
            sig_neg = jnp.where(z >= 0.0, ez, 1.0) / one_plus
            if layer == 0:
                lf_ref[:, cols] = logsig
                kk_ref[:, cols] = sig_neg.astype(BF16)
            else:
                lb = lb_all[:, cols]
                log_lb = jnp.log(lb)
                b2 = jnp.log1p(-lb) + logsig
                amax = jnp.maximum(log_lb, b2)
                lf_ref[:, cols] = amax + jnp.log(1.0 + jnp.exp(-jnp.abs(log_lb - b2)))
                kk_ref[:, cols] = ((1.0 - lb) * sig_neg).astype(BF16)
        _by_columns(h_ref, w_ref, epilogue)

    @pl.when(j == 2)
    def _():
        def epilogue(acc, cols):
            ib_ref[:, cols] = acc.astype(BF16)
        _by_columns(h_ref, w_ref, epilogue)

    @pl.when(j == 3)
    def _():
        def epilogue(acc, cols):
            gs_ref[:, cols] = (acc * _sigmoid(acc)).astype(BF16)
        _by_columns(h_ref, w_ref, epilogue)


def _inproj_b(h, w, lbp, *, layer, decode, tm):
    m, d = h.shape
    n_seg = 4
    if decode:
        assert m == tm
        w_spec = pl.BlockSpec((None, d, SEG), lambda i, j: (layer, 0, j + 2))
    else:
        w_spec = pl.BlockSpec((d, SEG), lambda i, j: (0, j))
    row_spec = pl.BlockSpec((tm, SEG), lambda i, j: (i, 0))
    out_shape = [jax.ShapeDtypeStruct((m, SEG), BF16),
                 jax.ShapeDtypeStruct((m, SEG), F32),
                 jax.ShapeDtypeStruct((m, SEG), BF16),
                 jax.ShapeDtypeStruct((m, SEG), BF16),
                 jax.ShapeDtypeStruct((m, SEG), BF16)]
    out_specs = [row_spec] * 5
    if decode:
        out_shape.append(jax.ShapeDtypeStruct((d, n_seg * SEG), BF16))
        out_specs.append(pl.BlockSpec((d, SEG), lambda i, j: (0, j)))
    return pl.pallas_call(
        functools.partial(_inproj_b_kernel, layer=layer, decode=decode),
        grid=(m // tm, n_seg),
        in_specs=[pl.BlockSpec((tm, d), lambda i, j: (i, 0)),
                  w_spec,
                  pl.BlockSpec(lbp.shape, lambda i, j: (0, 0))],
        out_specs=out_specs,
        out_shape=out_shape,
        compiler_params=pltpu.CompilerParams(
            dimension_semantics=("arbitrary", "arbitrary"), vmem_limit_bytes=VMEM_LIMIT),
        name="inproj_b_decode" if decode else "inproj_b",
    )(h, w, lbp)


def _level_table():
    t = np.arange(CHUNK)[:, None]
    s = np.arange(CHUNK)[None, :]
    x = t ^ s
    lev = np.floor(np.log2(np.maximum(x, 1))).astype(np.int32)
    lev = np.where(t == s, -1, lev)
    lev = np.where(t < s, -2, lev)
    return lev.astype(np.int32)


def _log_gap_to_block_boundary(b, level):
    half = 1 << level
    blk = 2 * half
    k = b.shape[1]
    if half >= SUBLANES:
        bb = b.reshape(CHUNK // blk, blk, k)
        r = bb[:, half - 1:half, :]
        gap = jnp.concatenate([r - bb[:, :half, :], bb[:, half:, :] - r], axis=1)
        return gap.reshape(CHUNK, k)
    b3 = b.reshape(CHUNK // SUBLANES, SUBLANES, k)
    sub = lax.broadcasted_iota(jnp.int32, (1, SUBLANES, k), 1)
    r = None
    for start in range(0, SUBLANES, blk):
        cand = jnp.broadcast_to(b3[:, start + half - 1:start + half, :], b3.shape)
        r = cand if r is None else jnp.where(sub >= start, cand, r)
    sign = jnp.where((sub & half) != 0, 1.0, -1.0)
    return ((b3 - r) * sign).reshape(CHUNK, k)


def _second_half_rows_from(a, b, level):
    half = 1 << level
    blk = 2 * half
    k = a.shape[1]
    if half >= SUBLANES:
        a3 = a.reshape(CHUNK // blk, blk, k)
        b3 = b.reshape(CHUNK // blk, blk, k)
        return jnp.concatenate([b3[:, :half, :], a3[:, half:, :]], axis=1).reshape(CHUNK, k)
    sub = lax.broadcasted_iota(jnp.int32, (1, SUBLANES, k), 1)
    a3 = a.reshape(CHUNK // SUBLANES, SUBLANES, k)
    b3 = b.reshape(CHUNK // SUBLANES, SUBLANES, k)
    return jnp.where((sub & half) != 0, a3, b3).reshape(CHUNK, k)


def _hgrn_prompt_kernel(qs_ref, lf_ref, kk_ref, ib_ref, gs_ref, gout_ref, s0t_ref, lev_ref,
                        ob_ref, sfin_ref, st_scr):
    c = pl.program_id(1)

    @pl.when(c == 0)
    def _():
        st_scr[...] = s0t_ref[0]

    lev = lev_ref[...]
    row = lax.broadcasted_iota(jnp.int32, (CHUNK, CHUNK), 0)
    col = lax.broadcasted_iota(jnp.int32, (CHUNK, CHUNK), 1)
    tri = jnp.where(row >= col, 1.0, 0.0).astype(BF16)

    odd_row = (lax.broadcasted_iota(jnp.int32, (CHUNK, HEAD_DIM), 0) & 1) == 1
    for hh in range(HEADS):
        cs = slice(hh * HEAD_DIM, (hh + 1) * HEAD_DIM)
        lf2 = lf_ref[:, cs] * LOG2E
        qb = qs_ref[:, cs]
        kb = kk_ref[:, cs]
        q = qb.astype(F32)
        kk = kb.astype(F32)
        v = ib_ref[:, cs]
        st = st_scr[hh]

        hi = lf2.astype(BF16)
        r1 = lf2 - hi.astype(F32)
        mid = r1.astype(BF16)
        lo = (r1 - mid.astype(F32)).astype(BF16)
        b = _dot(tri, hi) + _dot(tri, mid) + _dot(tri, lo)
        b_last = b[CHUNK - 1:CHUNK, :]

        o = _dot_nt((q * jnp.exp2(b)).astype(BF16), st.astype(BF16))

        a = jnp.where(lev == -1, _dot_nt(qb, kb), 0.0)
        for level in range(N_LEVELS):
            if level == 0:
                d = jnp.where(odd_row, lf2, 0.0)
            else:
                d = _log_gap_to_block_boundary(b, level)
            x = (_second_half_rows_from(q, kk, level) * jnp.exp2(d)).astype(BF16)
            a = jnp.where(lev == level, _dot_nt(x, x), a)
        o = o + _dot(a.astype(BF16), v)

        kd = (kk * jnp.exp2(b_last - b)).astype(BF16)
        st_scr[hh] = st * jnp.exp2(b_last) + _dot_tn(v, kd)

        on = o * lax.rsqrt(jnp.mean(o * o, axis=-1, keepdims=True) + EPS)
        ob_ref[:, cs] = (on * gout_ref[:, cs] * gs_ref[:, cs].astype(F32)).astype(BF16)

    @pl.when(c == pl.num_programs(1) - 1)
    def _():
        for hh in range(HEADS):
            sfin_ref[0, hh] = st_scr[hh].T


def _hgrn_prompt(qs, lf, kk, ib, gs, gout, s0t, lev, *, layer, batch, seq_len):
    m = qs.shape[0]
    nc = seq_len // CHUNK
    row_spec = pl.BlockSpec((CHUNK, SEG), lambda b, c: (b * nc + c, 0))
    state_spec = pl.BlockSpec((1, HEADS, HEAD_DIM, HEAD_DIM), lambda b, c: (b, 0, 0, 0))
    return pl.pallas_call(
        _hgrn_prompt_kernel,
        grid=(batch, nc),
        in_specs=[row_spec, row_spec, row_spec, row_spec, row_spec,
                  _layer_vec_spec(layer, SEG, 2),
                  state_spec,
                  pl.BlockSpec((CHUNK, CHUNK), lambda b, c: (0, 0))],
        out_specs=[row_spec, state_spec],
        out_shape=[jax.ShapeDtypeStruct((m, SEG), BF16),
                   jax.ShapeDtypeStruct((batch, HEADS, HEAD_DIM, HEAD_DIM), F32)],
        scratch_shapes=[pltpu.VMEM((HEADS, HEAD_DIM, HEAD_DIM), F32)],
        compiler_params=pltpu.CompilerParams(
            dimension_semantics=("arbitrary", "arbitrary"), vmem_limit_bytes=VMEM_LIMIT),
        name="hgrn_prompt",
    )(qs, lf, kk, ib, gs, gout, s0t, lev)


def _hgrn_decode_kernel(lft_ref, kkt_ref, qst_ref, ib_ref, gs_ref, gout_ref, s0_ref, *rest):
    ob_ref, snew_ref, o_scr = rest[-3:]
    tb = s0_ref.shape[0]
    for bb in range(tb):
        for hh in range(HEADS):
            ks = slice(hh * HEAD_DIM, (hh + 1) * HEAD_DIM)
            f_col = jnp.exp(lft_ref[0, ks, bb:bb + 1])
            k_col = kkt_ref[0, ks, bb:bb + 1].astype(F32)
            q_col = qst_ref[0, ks, bb:bb + 1].astype(F32)
            i_row = ib_ref[bb:bb + 1, ks].astype(F32)
            s_new = f_col * s0_ref[bb, hh] + k_col * i_row
            snew_ref[bb, hh] = s_new
            o_scr[bb:bb + 1, ks] = jnp.sum(q_col * s_new, axis=0, keepdims=True)
    for hh in range(HEADS):
        ks = slice(hh * HEAD_DIM, (hh + 1) * HEAD_DIM)
        o = o_scr[:, ks]
        on = o * lax.rsqrt(jnp.mean(o * o, axis=-1, keepdims=True) + EPS)
        ob_ref[:, ks] = (on * gout_ref[:, ks] * gs_ref[:, ks].astype(F32)).astype(BF16)


def _hgrn_decode(lft, kkt, qst, ib, gs, gout, s0_all, s_new_all, *, layer, tb):
    m = ib.shape[0]
    col_spec = pl.BlockSpec((1, SEG, tb), lambda i: (i, 0, 0))
    row_spec = pl.BlockSpec((tb, SEG), lambda i: (i, 0))
    state_spec = pl.BlockSpec((None, tb, HEADS, HEAD_DIM, HEAD_DIM), lambda i: (layer, i, 0, 0, 0))
    in_specs = [col_spec, col_spec, col_spec, row_spec, row_spec,
                _layer_vec_spec(layer, SEG, 1), state_spec]
    args = [lft, kkt, qst, ib, gs, gout, s0_all]
    aliases = {}
    if s_new_all is not None:
        in_specs.append(pl.BlockSpec(memory_space=pl.ANY))
        args.append(s_new_all)
        aliases = {len(args) - 1: 1}
    return pl.pallas_call(
        _hgrn_decode_kernel,
        grid=(m // tb,),
        in_specs=in_specs,
        out_specs=[row_spec, state_spec],
        out_shape=[jax.ShapeDtypeStruct((m, SEG), BF16),
                   jax.ShapeDtypeStruct(s0_all.shape, F32)],
        input_output_aliases=aliases,
        scratch_shapes=[pltpu.VMEM((tb, SEG), F32)],
        compiler_params=pltpu.CompilerParams(
            dimension_semantics=("arbitrary",), vmem_limit_bytes=VMEM_LIMIT),
        name="hgrn_decode",
    )(*args)


def _outproj_kernel(oa_ref, ob_ref, w_ref, x_ref, g_ref, o_ref):
    mix = _dot(oa_ref[...], w_ref[:SEG, :]) + _dot(ob_ref[...], w_ref[SEG:, :])
    o_ref[...] = x_ref[...] + _rms_norm(mix, g_ref[...])


def _outproj_cast_kernel(oa_ref, ob_ref, w_ref, x_ref, g_ref, o_ref, wb_ref, acc_scr):
    k = pl.program_id(0)
    wb = w_ref[...].astype(BF16)
    wb_ref[...] = wb

    @pl.when(k == 0)
    def _():
        acc_scr[...] = _dot(oa_ref[...], wb)

    @pl.when(k == 1)
    def _():
        o_ref[...] = x_ref[...] + _rms_norm(acc_scr[...] + _dot(ob_ref[...], wb), g_ref[...])


def _outproj_cast(oa, ob, w, x, g, *, layer):
    m, d = x.shape
    whole = lambda shape: pl.BlockSpec(shape, lambda k: (0,) * len(shape))
    return pl.pallas_call(
        _outproj_cast_kernel,
        grid=(2,),
        in_specs=[whole((m, SEG)), whole((m, SEG)),
                  pl.BlockSpec((None, SEG, d), lambda k: (layer, k, 0)),
                  whole((m, d)),
                  _layer_vec_spec(layer, d, 1)],
        out_specs=[whole((m, d)), pl.BlockSpec((SEG, d), lambda k: (k, 0))],
        out_shape=[jax.ShapeDtypeStruct((m, d), F32),
                   jax.ShapeDtypeStruct((2 * SEG, d), BF16)],
        scratch_shapes=[pltpu.VMEM((m, d), F32)],
        compiler_params=pltpu.CompilerParams(
            dimension_semantics=("arbitrary",), vmem_limit_bytes=VMEM_LIMIT),
        name="outproj_decode",
    )(oa, ob, w, x, g)


def _outproj(oa, ob, w, x, g, *, layer, tm):
    m, d = x.shape
    return pl.pallas_call(
        _outproj_kernel,
        grid=(m // tm,),
        in_specs=[pl.BlockSpec((tm, SEG), lambda i: (i, 0)),
                  pl.BlockSpec((tm, SEG), lambda i: (i, 0)),
                  pl.BlockSpec(w.shape, lambda i: (0, 0)),
                  pl.BlockSpec((tm, d), lambda i: (i, 0)),
                  _layer_vec_spec(layer, d, 1)],
        out_specs=pl.BlockSpec((tm, d), lambda i: (i, 0)),
        out_shape=jax.ShapeDtypeStruct((m, d), F32),
        compiler_params=pltpu.CompilerParams(
            dimension_semantics=("arbitrary",), vmem_limit_bytes=VMEM_LIMIT),
        name="outproj",
    )(oa, ob, w, x, g)


def _ffn_kernel(x_ref, gpre_ref, wg_ref, wu_ref, wd_ref, gpost_ref, o_ref, *rest, cast):
    h_scr = rest[-1]
    f = pl.program_id(1)
    nf = pl.num_programs(1)
    tm = x_ref.shape[0]
    if cast:
        for src, dst in zip((wg_ref, wu_ref, wd_ref), rest[:3]):
            dst[...] = src[...].astype(BF16)
        wg_ref, wu_ref, wd_ref = rest[:3]

    def contribution(rows):
        h = h_scr[rows, :]
        gate = _dot(h, wg_ref[...])
        up = _dot(h, wu_ref[...])
        t = (gate * _sigmoid(gate) * up).astype(BF16)
        return _dot(t, wd_ref[...])

    @pl.when(f == 0)
    def _():
        rc = min(tm, ROW_CHUNK)
        for r in range(tm // rc):
            rows = slice(r * rc, (r + 1) * rc)
            h_scr[rows, :] = _rms_norm(x_ref[rows, :], gpre_ref[...]).astype(BF16)
        o_ref[...] = contribution(slice(None))

    @pl.when(jnp.logical_and(f > 0, f < nf - 1))
    def _():
        o_ref[...] += contribution(slice(None))

    @pl.when(f == nf - 1)
    def _():
        rb = min(tm, FFN_LAST_ROW_BLOCK)
        for r in range(tm // rb):
            rows = slice(r * rb, (r + 1) * rb)
            acc = o_ref[rows, :] + contribution(rows)
            o_ref[rows, :] = x_ref[rows, :] + _rms_norm(acc, gpost_ref[...])


def _ffn(x, gpre, wg, wu, wd, gpost, *, layer, tm, tf, cast):
    m, d = x.shape
    dff = wg.shape[-1]
    assert dff // tf >= 2
    col_spec = pl.BlockSpec((d, tf), lambda i, f: (0, f))
    row_spec = pl.BlockSpec((tf, d), lambda i, f: (f, 0))
    out_specs = [pl.BlockSpec((tm, d), lambda i, f: (i, 0))]
    out_shape = [jax.ShapeDtypeStruct((m, d), F32)]
    if cast:
        assert m == tm
        w_specs = [pl.BlockSpec((None, d, tf), lambda i, f: (layer, 0, f)),
                   pl.BlockSpec((None, d, tf), lambda i, f: (layer, 0, f)),
                   pl.BlockSpec((None, tf, d), lambda i, f: (layer, f, 0))]
        out_specs += [col_spec, col_spec, row_spec]
        out_shape += [jax.ShapeDtypeStruct((d, dff), BF16), jax.ShapeDtypeStruct((d, dff), BF16),
                      jax.ShapeDtypeStruct((dff, d), BF16)]
    else:
        w_specs = [col_spec, col_spec, row_spec]
    res = pl.pallas_call(
        functools.partial(_ffn_kernel, cast=cast),
        grid=(m // tm, dff // tf),
        in_specs=[pl.BlockSpec((tm, d), lambda i, f: (i, 0)),
                  _layer_vec_spec(layer, d, 2),
                  *w_specs,
                  _layer_vec_spec(layer, d, 2)],
        out_specs=out_specs,
        out_shape=out_shape,
        scratch_shapes=[pltpu.VMEM((tm, d), BF16)],
        compiler_params=pltpu.CompilerParams(
            dimension_semantics=("arbitrary", "arbitrary"), vmem_limit_bytes=VMEM_LIMIT),
        name="ffn_decode" if cast else "ffn",
    )(x, gpre, wg, wu, wd, gpost)
    return res if cast else res[0]


PROMPT_TM = 1024
OUTPROJ_TM = 512
FFN_TF = 512
DECODE_TB = 8


def kernel(x_prompt, x_sample, state_hgrn, norm_mix_pre, norm_mix_post, w_in, ln_v_gain, ln_v_bias,
           spatial_w, spatial_b, lb_param, hgrn_out_gain, w_out, norm_ffn_pre, norm_ffn_post,
           w_gate, w_up, w_down):
    depth = w_in.shape[0]
    batch, seq_len, d = x_prompt.shape
    dec_batch, dec_seq, _ = x_sample.shape
    assert dec_seq == 1
    assert dec_batch % DECODE_TB == 0 and w_in.shape[2] == 6 * SEG and w_out.shape[1] == 2 * SEG

    lev = jnp.asarray(_level_table())
    vec = lambda a: a.reshape(depth, 1, -1)
    g_mix_pre, g_mix_post = vec(norm_mix_pre), vec(norm_mix_post)
    g_ffn_pre, g_ffn_post = vec(norm_ffn_pre), vec(norm_ffn_post)
    ln_g, ln_b, g_out = vec(ln_v_gain), vec(ln_v_bias), vec(hgrn_out_gain)
    spatial_bt = spatial_b.transpose(0, 2, 1)
    w00 = vec(jnp.repeat(spatial_w[:, :, 0, 0], HEAD_DIM, axis=1))
    b0 = vec(jnp.repeat(spatial_b[:, :, 0], HEAD_DIM, axis=1))

    xp = x_prompt.reshape(batch * seq_len, d)
    xs = x_sample.reshape(dec_batch, d)
    s0t_zero = jnp.zeros((batch, HEADS, HEAD_DIM, HEAD_DIM), F32)
    sp_out, vp_out, vs_out = [], [], []
    ss_all = None

    for l in range(depth):
        oa, vrows, hs, w_in_a = _inproj_a(
            xs, g_mix_pre, w_in, ln_g, ln_b, w00, b0, layer=l, decode=True, tm=dec_batch, seq_len=1)
        qs, lf, kk, ib, gs, w_in_b = _inproj_b(hs, w_in, lb_param, layer=l, decode=True, tm=dec_batch)
        cols = lambda a: a.reshape(dec_batch // DECODE_TB, DECODE_TB, SEG).transpose(0, 2, 1)
        ob, ss_all = _hgrn_decode(cols(lf), cols(kk), cols(qs), ib, gs, g_out, state_hgrn, ss_all,
                                  layer=l, tb=DECODE_TB)
        xs, w_out_b = _outproj_cast(oa, ob, w_out, xs, g_mix_post, layer=l)
        xs, w_gate_b, w_up_b, w_down_b = _ffn(xs, g_ffn_pre, w_gate, w_up, w_down, g_ffn_post,
                                              layer=l, tm=dec_batch, tf=FFN_TF, cast=True)
        vs_out.append(vrows.reshape(dec_batch, 1, SEG))
        oa, vrows, hp = _inproj_a(
            xp, g_mix_pre, w_in_a, ln_g, ln_b, spatial_w, spatial_bt,
            layer=l, decode=False, tm=PROMPT_TM, seq_len=seq_len)
        qs, lf, kk, ib, gs = _inproj_b(hp, w_in_b, lb_param, layer=l, decode=False, tm=PROMPT_TM)
        ob, s_fin = _hgrn_prompt(qs, lf, kk, ib, gs, g_out, s0t_zero, lev,
                                 layer=l, batch=batch, seq_len=seq_len)
        xp = _outproj(oa, ob, w_out_b, xp, g_mix_post, layer=l, tm=OUTPROJ_TM)
        xp = _ffn(xp, g_ffn_pre, w_gate_b, w_up_b, w_down_b, g_ffn_post,
                  layer=l, tm=PROMPT_TM, tf=FFN_TF, cast=False)
        sp_out.append(s_fin)
        vp_out.append(vrows.reshape(batch, CHUNK, SEG))

    return (xp.reshape(batch, seq_len, d), xs.reshape(dec_batch, 1, d),
            jnp.stack(sp_out), ss_all, jnp.stack(vp_out), jnp.stack(vs_out))
```

```python
import functools

import numpy as np
import jax
import jax.numpy as jnp
from jax import lax
from jax.experimental import pallas as pl
from jax.experimental.pallas import tpu as pltpu

F32 = jnp.float32
BF16 = jnp.bfloat16

EPS = 1e-6
LANES = 128
SUBLANES = 8
SEG = 1024
HEADS = 8
HEAD_DIM = SEG // HEADS
CHUNK = 128
N_LEVELS = 7
ROW_CHUNK = 256
COL_CHUNK = 256
V_ROW_BLOCK = 512
FFN_LAST_ROW_BLOCK = 512
LOG2E = np.float32(1.4426950408889634)
VMEM_LIMIT = 60 * 1024 * 1024


def _rms_norm(x, g):
    return x * lax.rsqrt(jnp.mean(x * x, axis=-1, keepdims=True) + EPS) * g


def _sigmoid(x):
    return 1.0 / (1.0 + jnp.exp(-x))


def _gelu(x):
    return 0.5 * x * (1.0 + lax.erf(x * np.float32(np.sqrt(0.5))))


def _dot(a, b):
    return jnp.dot(a, b, preferred_element_type=F32)


def _dot_nt(a, b):
    return lax.dot_general(a, b, (((1,), (1,)), ((), ())), preferred_element_type=F32)


def _dot_tn(a, b):
    return lax.dot_general(a, b, (((0,), (0,)), ((), ())), preferred_element_type=F32)


def _forget_lower_bound(lbp, layer):
    m = jnp.max(lbp, axis=0, keepdims=True)
    e = jnp.exp(lbp - m)
    sm = e / jnp.sum(e, axis=0, keepdims=True)
    cum = sm[0:1]
    for r in range(1, layer + 1):
        cum = cum + sm[r:r + 1]
    return cum - sm[0:1]


def _layer_vec_spec(layer, width, grid_rank):
    if grid_rank == 1:
        return pl.BlockSpec((None, 1, width), lambda i: (layer, 0, 0))
    return pl.BlockSpec((None, 1, width), lambda i, j: (layer, 0, 0))


def _by_columns(h_ref, w_ref, epilogue):
    for c in range(SEG // COL_CHUNK):
        cols = slice(c * COL_CHUNK, (c + 1) * COL_CHUNK)
        epilogue(_dot(h_ref[...], w_ref[:, cols]), cols)


def _inproj_a_kernel(x_ref, gpre_ref, w_ref, lng_ref, lnb_ref, ws_ref, bs_ref,
                     oa_ref, vrows_ref, h_ref, *rest, decode):
    wb_ref = rest[0] if decode else None
    u_scr = rest[-1]
    j = pl.program_id(1)
    tm = x_ref.shape[0]
    if decode:
        wb_ref[...] = w_ref[...].astype(BF16)
    wmat_ref = wb_ref if decode else w_ref

    @pl.when(j == 0)
    def _():
        rc = min(tm, ROW_CHUNK)
        for r in range(tm // rc):
            rows = slice(r * rc, (r + 1) * rc)
            h_ref[rows, :] = _rms_norm(x_ref[rows, :], gpre_ref[...]).astype(BF16)

        def epilogue(acc, cols):
            u_scr[:, cols] = _gelu(acc)
        _by_columns(h_ref, wmat_ref, epilogue)

    @pl.when(j == 1)
    def _():
        if not decode:
            row = lax.broadcasted_iota(jnp.int32, (CHUNK, CHUNK), 0)
            col = lax.broadcasted_iota(jnp.int32, (CHUNK, CHUNK), 1)
            causal = row >= col
        rb = min(tm, V_ROW_BLOCK)
        for r in range(tm // rb):
            rows = slice(r * rb, (r + 1) * rb)
            gv = _gelu(_dot(h_ref[rows, :], wmat_ref[...]))
            mu = jnp.mean(gv, axis=-1, keepdims=True)
            vc = gv - mu
            v = vc * lax.rsqrt(jnp.mean(vc * vc, axis=-1, keepdims=True) + EPS)
            v = v * lng_ref[...] + lnb_ref[...]
            if decode:
                vrows_ref[rows, :] = v
                oa_ref[rows, :] = (u_scr[rows, :] * (v * ws_ref[...] + bs_ref[...])).astype(BF16)
                continue
            if r == tm // rb - 1:
                vrows_ref[...] = v[rb - CHUNK:, :]
            vb = v.astype(BF16)
            for hh in range(HEADS):
                cs = slice(hh * HEAD_DIM, (hh + 1) * HEAD_DIM)
                w_h = jnp.where(causal, ws_ref[hh], 0.0).astype(BF16)
                b_h = bs_ref[:, hh:hh + 1]
                for c in range(rb // CHUNK):
                    s = _dot(w_h, vb[c * CHUNK:(c + 1) * CHUNK, cs]) + b_h
                    rs = slice(rows.start + c * CHUNK, rows.start + (c + 1) * CHUNK)
                    oa_ref[rs, cs] = (u_scr[rs, cs] * s).astype(BF16)


def _inproj_a(x, gpre, w, lng, lnb, ws, bs, *, layer, decode, tm, seq_len):
    m, d = x.shape
    if decode:
        assert m == tm
        vrows_shape, vrows_spec = (m, SEG), pl.BlockSpec((tm, SEG), lambda i, j: (i, 0))
        ws_spec = _layer_vec_spec(layer, SEG, 2)
        bs_spec = _layer_vec_spec(layer, SEG, 2)
        w_spec = pl.BlockSpec((None, d, SEG), lambda i, j: (layer, 0, j))
    else:
        assert seq_len % tm == 0 and tm % CHUNK == 0
        tiles_per_seq = seq_len // tm
        vrows_shape = (m // seq_len * CHUNK, SEG)
        vrows_spec = pl.BlockSpec((CHUNK, SEG), lambda i, j: (i // tiles_per_seq, 0))
        ws_spec = pl.BlockSpec((None, HEADS, CHUNK, CHUNK), lambda i, j: (layer, 0, 0, 0))
        bs_spec = pl.BlockSpec((None, CHUNK, HEADS), lambda i, j: (layer, 0, 0))
        w_spec = pl.BlockSpec((d, SEG), lambda i, j: (0, j))
    out_shape = [jax.ShapeDtypeStruct((m, SEG), BF16),
                 jax.ShapeDtypeStruct(vrows_shape, F32),
                 jax.ShapeDtypeStruct((m, d), BF16)]
    out_specs = [pl.BlockSpec((tm, SEG), lambda i, j: (i, 0)), vrows_spec,
                 pl.BlockSpec((tm, d), lambda i, j: (i, 0))]
    if decode:
        out_shape.append(jax.ShapeDtypeStruct((d, 2 * SEG), BF16))
        out_specs.append(pl.BlockSpec((d, SEG), lambda i, j: (0, j)))
    return pl.pallas_call(
        functools.partial(_inproj_a_kernel, decode=decode),
        grid=(m // tm, 2),
        in_specs=[pl.BlockSpec((tm, d), lambda i, j: (i, 0)),
                  _layer_vec_spec(layer, d, 2),
                  w_spec,
                  _layer_vec_spec(layer, SEG, 2), _layer_vec_spec(layer, SEG, 2), ws_spec, bs_spec],
        out_specs=out_specs,
        out_shape=out_shape,
        scratch_shapes=[pltpu.VMEM((tm, SEG), F32)],
        compiler_params=pltpu.CompilerParams(
            dimension_semantics=("arbitrary", "arbitrary"), vmem_limit_bytes=VMEM_LIMIT),
        name="inproj_a_decode" if decode else "inproj_a",
    )(x, gpre, w, lng, lnb, ws, bs)


def _inproj_b_kernel(h_ref, w_ref, lbp_ref, qs_ref, lf_ref, kk_ref, ib_ref, gs_ref, *rest,
                     layer, decode):
    j = pl.program_id(1)
    if decode:
        wb_ref = rest[0]
        wb_ref[...] = w_ref[...].astype(BF16)
        w_ref = wb_ref

    @pl.when(j == 0)
    def _():
        def epilogue(acc, cols):
            qs_ref[:, cols] = (acc * _sigmoid(acc)).astype(BF16)
        _by_columns(h_ref, w_ref, epilogue)

    @pl.when(j == 1)
    def _():
        lb_all = _forget_lower_bound(lbp_ref[...], layer)

        def epilogue(z, cols):
            ez = jnp.exp(-jnp.abs(z))
            one_plus = 1.0 + ez
            inv = 1.0 / one_plus
            logsig = jnp.minimum(z, 0.0) - jnp.log(one_plus)
            sig_neg = jnp.where(z >= 0.0, ez, 1.0) * inv
            if layer == 0:
                lf_ref[:, cols] = logsig
                kk_ref[:, cols] = sig_neg.astype(BF16)
            else:
                lb = lb_all[:, cols]
                f = lb + (1.0 - lb) * (jnp.where(z >= 0.0, 1.0, ez) * inv)
                lf_ref[:, cols] = jnp.maximum(jnp.log(f), logsig)
                kk_ref[:, cols] = ((1.0 - lb) * sig_neg).astype(BF16)
        _by_columns(h_ref, w_ref, epilogue)

    @pl.when(j == 2)
    def _():
        def epilogue(acc, cols):
            ib_ref[:, cols] = acc.astype(BF16)
        _by_columns(h_ref, w_ref, epilogue)

    @pl.when(j == 3)
    def _():
        def epilogue(acc, cols):
            gs_ref[:, cols] = (acc * _sigmoid(acc)).astype(BF16)
        _by_columns(h_ref, w_ref, epilogue)


def _inproj_b(h, w, lbp, *, layer, decode, tm):
    m, d = h.shape
    n_seg = 4
    if decode:
        assert m == tm
        w_spec = pl.BlockSpec((None, d, SEG), lambda i, j: (layer, 0, j + 2))
    else:
        w_spec = pl.BlockSpec((d, SEG), lambda i, j: (0, j))
    row_spec = pl.BlockSpec((tm, SEG), lambda i, j: (i, 0))
    out_shape = [jax.ShapeDtypeStruct((m, SEG), BF16),
                 jax.ShapeDtypeStruct((m, SEG), F32),
                 jax.ShapeDtypeStruct((m, SEG), BF16),
                 jax.ShapeDtypeStruct((m, SEG), BF16),
                 jax.ShapeDtypeStruct((m, SEG), BF16)]
    out_specs = [row_spec] * 5
    if decode:
        out_shape.append(jax.ShapeDtypeStruct((d, n_seg * SEG), BF16))
        out_specs.append(pl.BlockSpec((d, SEG), lambda i, j: (0, j)))
    return pl.pallas_call(
        functools.partial(_inproj_b_kernel, layer=layer, decode=decode),
        grid=(m // tm, n_seg),
        in_specs=[pl.BlockSpec((tm, d), lambda i, j: (i, 0)),
                  w_spec,
                  pl.BlockSpec(lbp.shape, lambda i, j: (0, 0))],
        out_specs=out_specs,
        out_shape=out_shape,
        compiler_params=pltpu.CompilerParams(
            dimension_semantics=("arbitrary", "arbitrary"), vmem_limit_bytes=VMEM_LIMIT),
        name="inproj_b_decode" if decode else "inproj_b",
    )(h, w, lbp)


def _level_table():
    t = np.arange(CHUNK)[:, None]
    s = np.arange(CHUNK)[None, :]
    x = t ^ s
    lev = np.floor(np.log2(np.maximum(x, 1))).astype(np.int32)
    lev = np.where(t == s, -1, lev)
    lev = np.where(t < s, -2, lev)
    return lev.astype(np.int32)


def _log_gap_to_block_boundary(b, level):
    half = 1 << level
    blk = 2 * half
    k = b.shape[1]
    if half >= SUBLANES:
        bb = b.reshape(CHUNK // blk, blk, k)
        r = bb[:, half - 1:half, :]
        gap = jnp.concatenate([r - bb[:, :half, :], bb[:, half:, :] - r], axis=1)
        return gap.reshape(CHUNK, k)
    b3 = b.reshape(CHUNK // SUBLANES, SUBLANES, k)
    sub = lax.broadcasted_iota(jnp.int32, (1, SUBLANES, k), 1)
    r = None
    for start in range(0, SUBLANES, blk):
        cand = jnp.broadcast_to(b3[:, start + half - 1:start + half, :], b3.shape)
        r = cand if r is None else jnp.where(sub >= start, cand, r)
    sign = jnp.where((sub & half) != 0, 1.0, -1.0)
    return ((b3 - r) * sign).reshape(CHUNK, k)


def _second_half_rows_from(a, b, level):
    half = 1 << level
    blk = 2 * half
    k = a.shape[1]
    if half >= SUBLANES:
        a3 = a.reshape(CHUNK // blk, blk, k)
        b3 = b.reshape(CHUNK // blk, blk, k)
        return jnp.concatenate([b3[:, :half, :], a3[:, half:, :]], axis=1).reshape(CHUNK, k)
    sub = lax.broadcasted_iota(jnp.int32, (1, SUBLANES, k), 1)
    a3 = a.reshape(CHUNK // SUBLANES, SUBLANES, k)
    b3 = b.reshape(CHUNK // SUBLANES, SUBLANES, k)
    return jnp.where((sub & half) != 0, a3, b3).reshape(CHUNK, k)


def _hgrn_prompt_kernel(qs_ref, lf_ref, kk_ref, ib_ref, gs_ref, gout_ref, s0t_ref, lev_ref,
                        ob_ref, sfin_ref, st_scr):
    c = pl.program_id(1)

    @pl.when(c == 0)
    def _():
        st_scr[...] = s0t_ref[0]

    lev = lev_ref[...]
    row = lax.broadcasted_iota(jnp.int32, (CHUNK, CHUNK), 0)
    col = lax.broadcasted_iota(jnp.int32, (CHUNK, CHUNK), 1)
    tri = jnp.where(row >= col, 1.0, 0.0).astype(BF16)

    odd_row = (lax.broadcasted_iota(jnp.int32, (CHUNK, HEAD_DIM), 0) & 1) == 1
    heads = [slice(hh * HEAD_DIM, (hh + 1) * HEAD_DIM) for hh in range(HEADS)]
    lf2, q, kk, b, o, a = [], [], [], [], [], []
    for hh, cs in enumerate(heads):
        lf2.append(lf_ref[:, cs] * LOG2E)
        hi = lf2[hh].astype(BF16)
        r1 = lf2[hh] - hi.astype(F32)
        mid = r1.astype(BF16)
        lo = (r1 - mid.astype(F32)).astype(BF16)
        b.append(_dot(tri, hi) + _dot(tri, mid) + _dot(tri, lo))
    for hh, cs in enumerate(heads):
        qb = qs_ref[:, cs]
        kb = kk_ref[:, cs]
        q.append(qb.astype(F32))
        kk.append(kb.astype(F32))
        o.append(_dot_nt((q[hh] * jnp.exp2(b[hh])).astype(BF16), st_scr[hh].astype(BF16)))
        a.append(jnp.where(lev == -1, _dot_nt(qb, kb), 0.0))

    for level in range(N_LEVELS):
        for hh in range(HEADS):
            if level == 0:
                d = jnp.where(odd_row, lf2[hh], 0.0)
            else:
                d = _log_gap_to_block_boundary(b[hh], level)
            x = (_second_half_rows_from(q[hh], kk[hh], level) * jnp.exp2(d)).astype(BF16)
            a[hh] = jnp.where(lev == level, _dot_nt(x, x), a[hh])

    for hh, cs in enumerate(heads):
        o[hh] = o[hh] + _dot(a[hh].astype(BF16), ib_ref[:, cs])
    for hh, cs in enumerate(heads):
        b_last = b[hh][CHUNK - 1:CHUNK, :]
        kd = (kk[hh] * jnp.exp2(b_last - b[hh])).astype(BF16)
        st_scr[hh] = st_scr[hh] * jnp.exp2(b_last) + _dot_tn(ib_ref[:, cs], kd)
    for hh, cs in enumerate(heads):
        on = o[hh] * lax.rsqrt(jnp.mean(o[hh] * o[hh], axis=-1, keepdims=True) + EPS)
        ob_ref[:, cs] = (on * gout_ref[:, cs] * gs_ref[:, cs].astype(F32)).astype(BF16)

    @pl.when(c == pl.num_programs(1) - 1)
    def _():
        for hh in range(HEADS):
            sfin_ref[0, hh] = st_scr[hh].T


def _hgrn_prompt(qs, lf, kk, ib, gs, gout, s0t, lev, *, layer, batch, seq_len):
    m = qs.shape[0]
    nc = seq_len // CHUNK
    row_spec = pl.BlockSpec((CHUNK, SEG), lambda b, c: (b * nc + c, 0))
    state_spec = pl.BlockSpec((1, HEADS, HEAD_DIM, HEAD_DIM), lambda b, c: (b, 0, 0, 0))
    return pl.pallas_call(
        _hgrn_prompt_kernel,
        grid=(batch, nc),
        in_specs=[row_spec, row_spec, row_spec, row_spec, row_spec,
                  _layer_vec_spec(layer, SEG, 2),
                  state_spec,
                  pl.BlockSpec((CHUNK, CHUNK), lambda b, c: (0, 0))],
        out_specs=[row_spec, state_spec],
        out_shape=[jax.ShapeDtypeStruct((m, SEG), BF16),
                   jax.ShapeDtypeStruct((batch, HEADS, HEAD_DIM, HEAD_DIM), F32)],
        scratch_shapes=[pltpu.VMEM((HEADS, HEAD_DIM, HEAD_DIM), F32)],
        compiler_params=pltpu.CompilerParams(
            dimension_semantics=("arbitrary", "arbitrary"), vmem_limit_bytes=VMEM_LIMIT),
        name="hgrn_prompt",
    )(qs, lf, kk, ib, gs, gout, s0t, lev)


def _hgrn_decode_kernel(lft_ref, kkt_ref, qst_ref, ib_ref, gs_ref, gout_ref, s0_ref, *rest):
    ob_ref, snew_ref, o_scr = rest[-3:]
    tb = s0_ref.shape[0]
    for bb in range(tb):
        for hh in range(HEADS):
            ks = slice(hh * HEAD_DIM, (hh + 1) * HEAD_DIM)
            f_col = jnp.exp(lft_ref[0, ks, bb:bb + 1])
            k_col = kkt_ref[0, ks, bb:bb + 1].astype(F32)
            q_col = qst_ref[0, ks, bb:bb + 1].astype(F32)
            i_row = ib_ref[bb:bb + 1, ks].astype(F32)
            s_new = f_col * s0_ref[bb, hh] + k_col * i_row
            snew_ref[bb, hh] = s_new
            o_scr[bb:bb + 1, ks] = jnp.sum(q_col * s_new, axis=0, keepdims=True)
    for hh in range(HEADS):
        ks = slice(hh * HEAD_DIM, (hh + 1) * HEAD_DIM)
        o = o_scr[:, ks]
        on = o * lax.rsqrt(jnp.mean(o * o, axis=-1, keepdims=True) + EPS)
        ob_ref[:, ks] = (on * gout_ref[:, ks] * gs_ref[:, ks].astype(F32)).astype(BF16)


def _hgrn_decode(lft, kkt, qst, ib, gs, gout, s0_all, s_new_all, *, layer, tb):
    m = ib.shape[0]
    col_spec = pl.BlockSpec((1, SEG, tb), lambda i: (i, 0, 0))
    row_spec = pl.BlockSpec((tb, SEG), lambda i: (i, 0))
    state_spec = pl.BlockSpec((None, tb, HEADS, HEAD_DIM, HEAD_DIM), lambda i: (layer, i, 0, 0, 0))
    in_specs = [col_spec, col_spec, col_spec, row_spec, row_spec,
                _layer_vec_spec(layer, SEG, 1), state_spec]
    args = [lft, kkt, qst, ib, gs, gout, s0_all]
    aliases = {}
    if s_new_all is not None:
        in_specs.append(pl.BlockSpec(memory_space=pl.ANY))
        args.append(s_new_all)
        aliases = {len(args) - 1: 1}
    return pl.pallas_call(
        _hgrn_decode_kernel,
        grid=(m // tb,),
        in_specs=in_specs,
        out_specs=[row_spec, state_spec],
        out_shape=[jax.ShapeDtypeStruct((m, SEG), BF16),
                   jax.ShapeDtypeStruct(s0_all.shape, F32)],
        input_output_aliases=aliases,
        scratch_shapes=[pltpu.VMEM((tb, SEG), F32)],
        compiler_params=pltpu.CompilerParams(
            dimension_semantics=("arbitrary",), vmem_limit_bytes=VMEM_LIMIT),
        name="hgrn_decode",
    )(*args)


def _outproj_kernel(oa_ref, ob_ref, w_ref, x_ref, g_ref, o_ref):
    mix = _dot(oa_ref[...], w_ref[:SEG, :]) + _dot(ob_ref[...], w_ref[SEG:, :])
    o_ref[...] = x_ref[...] + _rms_norm(mix, g_ref[...])


def _outproj_cast_kernel(oa_ref, ob_ref, w_ref, x_ref, g_ref, o_ref, wb_ref, acc_scr):
    k = pl.program_id(0)
    wb = w_ref[...].astype(BF16)
    wb_ref[...] = wb

    @pl.when(k == 0)
    def _():
        acc_scr[...] = _dot(oa_ref[...], wb)

    @pl.when(k == 1)
    def _():
        o_ref[...] = x_ref[...] + _rms_norm(acc_scr[...] + _dot(ob_ref[...], wb), g_ref[...])


def _outproj_cast(oa, ob, w, x, g, *, layer):
    m, d = x.shape
    whole = lambda shape: pl.BlockSpec(shape, lambda k: (0,) * len(shape))
    return pl.pallas_call(
        _outproj_cast_kernel,
        grid=(2,),
        in_specs=[whole((m, SEG)), whole((m, SEG)),
                  pl.BlockSpec((None, SEG, d), lambda k: (layer, k, 0)),
                  whole((m, d)),
                  _layer_vec_spec(layer, d, 1)],
        out_specs=[whole((m, d)), pl.BlockSpec((SEG, d), lambda k: (k, 0))],
        out_shape=[jax.ShapeDtypeStruct((m, d), F32),
                   jax.ShapeDtypeStruct((2 * SEG, d), BF16)],
        scratch_shapes=[pltpu.VMEM((m, d), F32)],
        compiler_params=pltpu.CompilerParams(
            dimension_semantics=("arbitrary",), vmem_limit_bytes=VMEM_LIMIT),
        name="outproj_decode",
    )(oa, ob, w, x, g)


def _outproj(oa, ob, w, x, g, *, layer, tm):
    m, d = x.shape
    return pl.pallas_call(
        _outproj_kernel,
        grid=(m // tm,),
        in_specs=[pl.BlockSpec((tm, SEG), lambda i: (i, 0)),
                  pl.BlockSpec((tm, SEG), lambda i: (i, 0)),
                  pl.BlockSpec(w.shape, lambda i: (0, 0)),
                  pl.BlockSpec((tm, d), lambda i: (i, 0)),
                  _layer_vec_spec(layer, d, 1)],
        out_specs=pl.BlockSpec((tm, d), lambda i: (i, 0)),
        out_shape=jax.ShapeDtypeStruct((m, d), F32),
        compiler_params=pltpu.CompilerParams(
            dimension_semantics=("arbitrary",), vmem_limit_bytes=VMEM_LIMIT),
        name="outproj",
    )(oa, ob, w, x, g)


def _ffn_kernel(x_ref, gpre_ref, wg_ref, wu_ref, wd_ref, gpost_ref, o_ref, *rest, cast):
    h_scr = rest[-1]
    f = pl.program_id(1)
    nf = pl.num_programs(1)
    tm = x_ref.shape[0]
    if cast:
        for src, dst in zip((wg_ref, wu_ref, wd_ref), rest[:3]):
            dst[...] = src[...].astype(BF16)
        wg_ref, wu_ref, wd_ref = rest[:3]

    def contribution(rows):
        h = h_scr[rows, :]
        gate = _dot(h, wg_ref[...])
        up = _dot(h, wu_ref[...])
        t = (gate * _sigmoid(gate) * up).astype(BF16)
        return _dot(t, wd_ref[...])

    @pl.when(f == 0)
    def _():
        rc = min(tm, ROW_CHUNK)
        for r in range(tm // rc):
            rows = slice(r * rc, (r + 1) * rc)
            h_scr[rows, :] = _rms_norm(x_ref[rows, :], gpre_ref[...]).astype(BF16)
        o_ref[...] = contribution(slice(None))

    @pl.when(jnp.logical_and(f > 0, f < nf - 1))
    def _():
        o_ref[...] += contribution(slice(None))

    @pl.when(f == nf - 1)
    def _():
        rb = min(tm, FFN_LAST_ROW_BLOCK)
        for r in range(tm // rb):
            rows = slice(r * rb, (r + 1) * rb)
            acc = o_ref[rows, :] + contribution(rows)
            o_ref[rows, :] = x_ref[rows, :] + _rms_norm(acc, gpost_ref[...])


def _ffn(x, gpre, wg, wu, wd, gpost, *, layer, tm, tf, cast):
    m, d = x.shape
    dff = wg.shape[-1]
    assert dff // tf >= 2
    col_spec = pl.BlockSpec((d, tf), lambda i, f: (0, f))
    row_spec = pl.BlockSpec((tf, d), lambda i, f: (f, 0))
    out_specs = [pl.BlockSpec((tm, d), lambda i, f: (i, 0))]
    out_shape = [jax.ShapeDtypeStruct((m, d), F32)]
    if cast:
        assert m == tm
        w_specs = [pl.BlockSpec((None, d, tf), lambda i, f: (layer, 0, f)),
                   pl.BlockSpec((None, d, tf), lambda i, f: (layer, 0, f)),
                   pl.BlockSpec((None, tf, d), lambda i, f: (layer, f, 0))]
        out_specs += [col_spec, col_spec, row_spec]
        out_shape += [jax.ShapeDtypeStruct((d, dff), BF16), jax.ShapeDtypeStruct((d, dff), BF16),
                      jax.ShapeDtypeStruct((dff, d), BF16)]
    else:
        w_specs = [col_spec, col_spec, row_spec]
    res = pl.pallas_call(
        functools.partial(_ffn_kernel, cast=cast),
        grid=(m // tm, dff // tf),
        in_specs=[pl.BlockSpec((tm, d), lambda i, f: (i, 0)),
                  _layer_vec_spec(layer, d, 2),
                  *w_specs,
                  _layer_vec_spec(layer, d, 2)],
        out_specs=out_specs,
        out_shape=out_shape,
        scratch_shapes=[pltpu.VMEM((tm, d), BF16)],
        compiler_params=pltpu.CompilerParams(
            dimension_semantics=("arbitrary", "arbitrary"), vmem_limit_bytes=VMEM_LIMIT),
        name="ffn_decode" if cast else "ffn",
    )(x, gpre, wg, wu, wd, gpost)
    return res if cast else res[0]


PROMPT_TM = 1024
OUTPROJ_TM = 512
FFN_TF = 512
DECODE_TB = 8


def kernel(x_prompt, x_sample, state_hgrn, norm_mix_pre, norm_mix_post, w_in, ln_v_gain, ln_v_bias,
           spatial_w, spatial_b, lb_param, hgrn_out_gain, w_out, norm_ffn_pre, norm_ffn_post,
           w_gate, w_up, w_down):
    depth = w_in.shape[0]
    batch, seq_len, d = x_prompt.shape
    dec_batch, dec_seq, _ = x_sample.shape
    assert dec_seq == 1
    assert dec_batch % DECODE_TB == 0 and w_in.shape[2] == 6 * SEG and w_out.shape[1] == 2 * SEG

    lev = jnp.asarray(_level_table())
    vec = lambda a: a.reshape(depth, 1, -1)
    g_mix_pre, g_mix_post = vec(norm_mix_pre), vec(norm_mix_post)
    g_ffn_pre, g_ffn_post = vec(norm_ffn_pre), vec(norm_ffn_post)
    ln_g, ln_b, g_out = vec(ln_v_gain), vec(ln_v_bias), vec(hgrn_out_gain)
    spatial_bt = spatial_b.transpose(0, 2, 1)
    w00 = vec(jnp.repeat(spatial_w[:, :, 0, 0], HEAD_DIM, axis=1))
    b0 = vec(jnp.repeat(spatial_b[:, :, 0], HEAD_DIM, axis=1))

    xp = x_prompt.reshape(batch * seq_len, d)
    xs = x_sample.reshape(dec_batch, d)
    s0t_zero = jnp.zeros((batch, HEADS, HEAD_DIM, HEAD_DIM), F32)
    sp_out, vp_out, vs_out = [], [], []
    ss_all = None

    for l in range(depth):
        oa, vrows, hs, w_in_a = _inproj_a(
            xs, g_mix_pre, w_in, ln_g, ln_b, w00, b0, layer=l, decode=True, tm=dec_batch, seq_len=1)
        qs, lf, kk, ib, gs, w_in_b = _inproj_b(hs, w_in, lb_param, layer=l, decode=True, tm=dec_batch)
        cols = lambda a: a.reshape(dec_batch // DECODE_TB, DECODE_TB, SEG).transpose(0, 2, 1)
        ob, ss_all = _hgrn_decode(cols(lf), cols(kk), cols(qs), ib, gs, g_out, state_hgrn, ss_all,
                                  layer=l, tb=DECODE_TB)
        xs, w_out_b = _outproj_cast(oa, ob, w_out, xs, g_mix_post, layer=l)
        xs, w_gate_b, w_up_b, w_down_b = _ffn(xs, g_ffn_pre, w_gate, w_up, w_down, g_ffn_post,
                                              layer=l, tm=dec_batch, tf=FFN_TF, cast=True)
        vs_out.append(vrows.reshape(dec_batch, 1, SEG))
        oa, vrows, hp = _inproj_a(
            xp, g_mix_pre, w_in_a, ln_g, ln_b, spatial_w, spatial_bt,
            layer=l, decode=False, tm=PROMPT_TM, seq_len=seq_len)
        qs, lf, kk, ib, gs = _inproj_b(hp, w_in_b, lb_param, layer=l, decode=False, tm=PROMPT_TM)
        ob, s_fin = _hgrn_prompt(qs, lf, kk, ib, gs, g_out, s0t_zero, lev,
                                 layer=l, batch=batch, seq_len=seq_len)
        xp = _outproj(oa, ob, w_out_b, xp, g_mix_post, layer=l, tm=OUTPROJ_TM)
        xp = _ffn(xp, g_ffn_pre, w_gate_b, w_up_b, w_down_b, g_ffn_post,
                  layer=l, tm=PROMPT_TM, tf=FFN_TF, cast=False)
        sp_out.append(s_fin)
        vp_out.append(vrows.reshape(batch, CHUNK, SEG))

    return (xp.reshape(batch, seq_len, d), xs.reshape(dec_batch, 1, d),
            jnp.stack(sp_out), ss_all, jnp.stack(vp_out), jnp.stack(vs_out))
```

```python
import functools

import numpy as np
import jax
import jax.numpy as jnp
from jax import lax
from jax.experimental import pallas as pl
from jax.experimental.pallas import tpu as pltpu

F32 = jnp.float32
BF16 = jnp.bfloat16

EPS = 1e-6
LANES = 128
SUBLANES = 8
SEG = 1024
HEADS = 8
HEAD_DIM = SEG // HEADS
CHUNK = 128
N_LEVELS = 7
HGRN_STEP_CHUNKS = 4
ROW_CHUNK = 256
COL_CHUNK = 256
V_ROW_BLOCK = 512
FFN_LAST_ROW_BLOCK = 512
LOG2E = np.float32(1.4426950408889634)
VMEM_LIMIT = 60 * 1024 * 1024


def _rms_norm(x, g):
    return x * lax.rsqrt(jnp.mean(x * x, axis=-1, keepdims=True) + EPS) * g


def _sigmoid(x):
    return 1.0 / (1.0 + jnp.exp(-x))


def _gelu(x):
    return 0.5 * x * (1.0 + lax.erf(x * np.float32(np.sqrt(0.5))))


def _dot(a, b):
    return jnp.dot(a, b, preferred_element_type=F32)


def _dot_nt(a, b):
    return lax.dot_general(a, b, (((1,), (1,)), ((), ())), preferred_element_type=F32)


def _dot_tn(a, b):
    return lax.dot_general(a, b, (((0,), (0,)), ((), ())), preferred_element_type=F32)


def _forget_lower_bound(lbp, layer):
    m = jnp.max(lbp, axis=0, keepdims=True)
    e = jnp.exp(lbp - m)
    sm = e / jnp.sum(e, axis=0, keepdims=True)
    cum = sm[0:1]
    for r in range(1, layer + 1):
        cum = cum + sm[r:r + 1]
    return cum - sm[0:1]


def _layer_vec_spec(layer, width, grid_rank):
    if grid_rank == 1:
        return pl.BlockSpec((None, 1, width), lambda i: (layer, 0, 0))
    return pl.BlockSpec((None, 1, width), lambda i, j: (layer, 0, 0))


def _by_columns(h_ref, w_ref, epilogue):
    for c in range(SEG // COL_CHUNK):
        cols = slice(c * COL_CHUNK, (c + 1) * COL_CHUNK)
        epilogue(_dot(h_ref[...], w_ref[:, cols]), cols)


def _inproj_a_kernel(x_ref, gpre_ref, w_ref, lng_ref, lnb_ref, ws_ref, bs_ref,
                     oa_ref, vrows_ref, h_ref, *rest, decode):
    wb_ref = rest[0] if decode else None
    u_scr = rest[-1]
    j = pl.program_id(1)
    tm = x_ref.shape[0]
    if decode:
        wb_ref[...] = w_ref[...].astype(BF16)
    wmat_ref = wb_ref if decode else w_ref

    @pl.when(j == 0)
    def _():
        rc = min(tm, ROW_CHUNK)
        for r in range(tm // rc):
            rows = slice(r * rc, (r + 1) * rc)
            h_ref[rows, :] = _rms_norm(x_ref[rows, :], gpre_ref[...]).astype(BF16)

        def epilogue(acc, cols):
            u_scr[:, cols] = _gelu(acc)
        _by_columns(h_ref, wmat_ref, epilogue)

    @pl.when(j == 1)
    def _():
        if not decode:
            row = lax.broadcasted_iota(jnp.int32, (CHUNK, CHUNK), 0)
            col = lax.broadcasted_iota(jnp.int32, (CHUNK, CHUNK), 1)
            causal = row >= col
        rb = min(tm, V_ROW_BLOCK)
        for r in range(tm // rb):
            rows = slice(r * rb, (r + 1) * rb)
            gv = _gelu(_dot(h_ref[rows, :], wmat_ref[...]))
            mu = jnp.mean(gv, axis=-1, keepdims=True)
            vc = gv - mu
            v = vc * lax.rsqrt(jnp.mean(vc * vc, axis=-1, keepdims=True) + EPS)
            v = v * lng_ref[...] + lnb_ref[...]
            if decode:
                vrows_ref[rows, :] = v
                oa_ref[rows, :] = (u_scr[rows, :] * (v * ws_ref[...] + bs_ref[...])).astype(BF16)
                continue
            if r == tm // rb - 1:
                vrows_ref[...] = v[rb - CHUNK:, :]
            vb = v.astype(BF16)
            for hh in range(HEADS):
                cs = slice(hh * HEAD_DIM, (hh + 1) * HEAD_DIM)
                w_h = jnp.where(causal, ws_ref[hh], 0.0).astype(BF16)
                b_h = bs_ref[:, hh:hh + 1]
                for c in range(rb // CHUNK):
                    s = _dot(w_h, vb[c * CHUNK:(c + 1) * CHUNK, cs]) + b_h
                    rs = slice(rows.start + c * CHUNK, rows.start + (c + 1) * CHUNK)
                    oa_ref[rs, cs] = (u_scr[rs, cs] * s).astype(BF16)


def _inproj_a(x, gpre, w, lng, lnb, ws, bs, *, layer, decode, tm, seq_len):
    m, d = x.shape
    if decode:
        assert m == tm
        vrows_shape, vrows_spec = (m, SEG), pl.BlockSpec((tm, SEG), lambda i, j: (i, 0))
        ws_spec = _layer_vec_spec(layer, SEG, 2)
        bs_spec = _layer_vec_spec(layer, SEG, 2)
        w_spec = pl.BlockSpec((None, d, SEG), lambda i, j: (layer, 0, j))
    else:
        assert seq_len % tm == 0 and tm % CHUNK == 0
        tiles_per_seq = seq_len // tm
        vrows_shape = (m // seq_len * CHUNK, SEG)
        vrows_spec = pl.BlockSpec((CHUNK, SEG), lambda i, j: (i // tiles_per_seq, 0))
        ws_spec = pl.BlockSpec((None, HEADS, CHUNK, CHUNK), lambda i, j: (layer, 0, 0, 0))
        bs_spec = pl.BlockSpec((None, CHUNK, HEADS), lambda i, j: (layer, 0, 0))
        w_spec = pl.BlockSpec((d, SEG), lambda i, j: (0, j))
    out_shape = [jax.ShapeDtypeStruct((m, SEG), BF16),
                 jax.ShapeDtypeStruct(vrows_shape, F32),
                 jax.ShapeDtypeStruct((m, d), BF16)]
    out_specs = [pl.BlockSpec((tm, SEG), lambda i, j: (i, 0)), vrows_spec,
                 pl.BlockSpec((tm, d), lambda i, j: (i, 0))]
    if decode:
        out_shape.append(jax.ShapeDtypeStruct((d, 2 * SEG), BF16))
        out_specs.append(pl.BlockSpec((d, SEG), lambda i, j: (0, j)))
    return pl.pallas_call(
        functools.partial(_inproj_a_kernel, decode=decode),
        grid=(m // tm, 2),
        in_specs=[pl.BlockSpec((tm, d), lambda i, j: (i, 0)),
                  _layer_vec_spec(layer, d, 2),
                  w_spec,
                  _layer_vec_spec(layer, SEG, 2), _layer_vec_spec(layer, SEG, 2), ws_spec, bs_spec],
        out_specs=out_specs,
        out_shape=out_shape,
        scratch_shapes=[pltpu.VMEM((tm, SEG), F32)],
        compiler_params=pltpu.CompilerParams(
            dimension_semantics=("arbitrary", "arbitrary"), vmem_limit_bytes=VMEM_LIMIT),
        name="inproj_a_decode" if decode else "inproj_a",
    )(x, gpre, w, lng, lnb, ws, bs)


def _inproj_b_kernel(h_ref, w_ref, lbp_ref, qs_ref, lf_ref, kk_ref, ib_ref, gs_ref, *rest,
                     layer, decode):
    j = pl.program_id(1)
    if decode:
        wb_ref = rest[0]
        wb_ref[...] = w_ref[...].astype(BF16)
        w_ref = wb_ref

    @pl.when(j == 0)
    def _():
        def epilogue(acc, cols):
            qs_ref[:, cols] = (acc * _sigmoid(acc)).astype(BF16)
        _by_columns(h_ref, w_ref, epilogue)

    @pl.when(j == 1)
    def _():
        lb_all = _forget_lower_bound(lbp_ref[...], layer)

        def epilogue(z, cols):
            ez = jnp.exp(-jnp.abs(z))
            one_plus = 1.0 + ez
            inv = 1.0 / one_plus
            logsig = jnp.minimum(z, 0.0) - jnp.log(one_plus)
            sig_neg = jnp.where(z >= 0.0, ez, 1.0) * inv
            if layer == 0:
                lf_ref[:, cols] = logsig
                kk_ref[:, cols] = sig_neg.astype(BF16)
            else:
                lb = lb_all[:, cols]
                f = lb + (1.0 - lb) * (jnp.where(z >= 0.0, 1.0, ez) * inv)
                lf_ref[:, cols] = jnp.maximum(jnp.log(f), logsig)
                kk_ref[:, cols] = ((1.0 - lb) * sig_neg).astype(BF16)
        _by_columns(h_ref, w_ref, epilogue)

    @pl.when(j == 2)
    def _():
        def epilogue(acc, cols):
            ib_ref[:, cols] = acc.astype(BF16)
        _by_columns(h_ref, w_ref, epilogue)

    @pl.when(j == 3)
    def _():
        def epilogue(acc, cols):
            gs_ref[:, cols] = (acc * _sigmoid(acc)).astype(BF16)
        _by_columns(h_ref, w_ref, epilogue)


def _inproj_b(h, w, lbp, *, layer, decode, tm):
    m, d = h.shape
    n_seg = 4
    if decode:
        assert m == tm
        w_spec = pl.BlockSpec((None, d, SEG), lambda i, j: (layer, 0, j + 2))
    else:
        w_spec = pl.BlockSpec((d, SEG), lambda i, j: (0, j))
    row_spec = pl.BlockSpec((tm, SEG), lambda i, j: (i, 0))
    out_shape = [jax.ShapeDtypeStruct((m, SEG), BF16),
                 jax.ShapeDtypeStruct((m, SEG), F32),
                 jax.ShapeDtypeStruct((m, SEG), BF16),
                 jax.ShapeDtypeStruct((m, SEG), BF16),
                 jax.ShapeDtypeStruct((m, SEG), BF16)]
    out_specs = [row_spec] * 5
    if decode:
        out_shape.append(jax.ShapeDtypeStruct((d, n_seg * SEG), BF16))
        out_specs.append(pl.BlockSpec((d, SEG), lambda i, j: (0, j)))
    return pl.pallas_call(
        functools.partial(_inproj_b_kernel, layer=layer, decode=decode),
        grid=(m // tm, n_seg),
        in_specs=[pl.BlockSpec((tm, d), lambda i, j: (i, 0)),
                  w_spec,
                  pl.BlockSpec(lbp.shape, lambda i, j: (0, 0))],
        out_specs=out_specs,
        out_shape=out_shape,
        compiler_params=pltpu.CompilerParams(
            dimension_semantics=("arbitrary", "arbitrary"), vmem_limit_bytes=VMEM_LIMIT),
        name="inproj_b_decode" if decode else "inproj_b",
    )(h, w, lbp)


def _level_table():
    t = np.arange(CHUNK)[:, None]
    s = np.arange(CHUNK)[None, :]
    x = t ^ s
    lev = np.floor(np.log2(np.maximum(x, 1))).astype(np.int32)
    lev = np.where(t == s, -1, lev)
    lev = np.where(t < s, -2, lev)
    return lev.astype(np.int32)


def _log_gap_to_block_boundary(b, level):
    half = 1 << level
    blk = 2 * half
    k = b.shape[1]
    if half >= SUBLANES:
        bb = b.reshape(CHUNK // blk, blk, k)
        r = bb[:, half - 1:half, :]
        gap = jnp.concatenate([r - bb[:, :half, :], bb[:, half:, :] - r], axis=1)
        return gap.reshape(CHUNK, k)
    b3 = b.reshape(CHUNK // SUBLANES, SUBLANES, k)
    sub = lax.broadcasted_iota(jnp.int32, (1, SUBLANES, k), 1)
    r = None
    for start in range(0, SUBLANES, blk):
        cand = jnp.broadcast_to(b3[:, start + half - 1:start + half, :], b3.shape)
        r = cand if r is None else jnp.where(sub >= start, cand, r)
    sign = jnp.where((sub & half) != 0, 1.0, -1.0)
    return ((b3 - r) * sign).reshape(CHUNK, k)


def _second_half_rows_from(a, b, level):
    half = 1 << level
    blk = 2 * half
    k = a.shape[1]
    if half >= SUBLANES:
        a3 = a.reshape(CHUNK // blk, blk, k)
        b3 = b.reshape(CHUNK // blk, blk, k)
        return jnp.concatenate([b3[:, :half, :], a3[:, half:, :]], axis=1).reshape(CHUNK, k)
    sub = lax.broadcasted_iota(jnp.int32, (1, SUBLANES, k), 1)
    a3 = a.reshape(CHUNK // SUBLANES, SUBLANES, k)
    b3 = b.reshape(CHUNK // SUBLANES, SUBLANES, k)
    return jnp.where((sub & half) != 0, a3, b3).reshape(CHUNK, k)


def _hgrn_chunk(qs_ref, lf_ref, kk_ref, ib_ref, gs_ref, gout_ref, ob_ref, st_scr, lev, tri):
    odd_row = (lax.broadcasted_iota(jnp.int32, (CHUNK, HEAD_DIM), 0) & 1) == 1
    heads = [slice(hh * HEAD_DIM, (hh + 1) * HEAD_DIM) for hh in range(HEADS)]
    lf2, q, kk, b, o, a = [], [], [], [], [], []
    for hh, cs in enumerate(heads):
        lf2.append(lf_ref[:, cs] * LOG2E)
        hi = lf2[hh].astype(BF16)
        r1 = lf2[hh] - hi.astype(F32)
        mid = r1.astype(BF16)
        lo = (r1 - mid.astype(F32)).astype(BF16)
        b.append(_dot(tri, hi) + _dot(tri, mid) + _dot(tri, lo))
    for hh, cs in enumerate(heads):
        qb = qs_ref[:, cs]
        kb = kk_ref[:, cs]
        q.append(qb.astype(F32))
        kk.append(kb.astype(F32))
        o.append(_dot_nt((q[hh] * jnp.exp2(b[hh])).astype(BF16), st_scr[hh].astype(BF16)))
        a.append(jnp.where(lev == -1, _dot_nt(qb, kb), 0.0))

    for level in range(N_LEVELS):
        for hh in range(HEADS):
            if level == 0:
                d = jnp.where(odd_row, lf2[hh], 0.0)
            else:
                d = _log_gap_to_block_boundary(b[hh], level)
            x = (_second_half_rows_from(q[hh], kk[hh], level) * jnp.exp2(d)).astype(BF16)
            a[hh] = jnp.where(lev == level, _dot_nt(x, x), a[hh])

    for hh, cs in enumerate(heads):
        o[hh] = o[hh] + _dot(a[hh].astype(BF16), ib_ref[:, cs])
    for hh, cs in enumerate(heads):
        b_last = b[hh][CHUNK - 1:CHUNK, :]
        kd = (kk[hh] * jnp.exp2(b_last - b[hh])).astype(BF16)
        st_scr[hh] = st_scr[hh] * jnp.exp2(b_last) + _dot_tn(ib_ref[:, cs], kd)
    for hh, cs in enumerate(heads):
        on = o[hh] * lax.rsqrt(jnp.mean(o[hh] * o[hh], axis=-1, keepdims=True) + EPS)
        ob_ref[:, cs] = (on * gout_ref[:, cs] * gs_ref[:, cs].astype(F32)).astype(BF16)


def _hgrn_prompt_kernel(qs_ref, lf_ref, kk_ref, ib_ref, gs_ref, gout_ref, s0t_ref, lev_ref,
                        ob_ref, sfin_ref, st_scr):
    c = pl.program_id(1)

    @pl.when(c == 0)
    def _():
        st_scr[...] = s0t_ref[0]

    lev = lev_ref[...]
    row = lax.broadcasted_iota(jnp.int32, (CHUNK, CHUNK), 0)
    col = lax.broadcasted_iota(jnp.int32, (CHUNK, CHUNK), 1)
    tri = jnp.where(row >= col, 1.0, 0.0).astype(BF16)

    for ci in range(lf_ref.shape[0] // CHUNK):
        rows = pl.ds(ci * CHUNK, CHUNK)
        _hgrn_chunk(qs_ref.at[rows], lf_ref.at[rows], kk_ref.at[rows], ib_ref.at[rows],
                    gs_ref.at[rows], gout_ref, ob_ref.at[rows], st_scr, lev, tri)

    @pl.when(c == pl.num_programs(1) - 1)
    def _():
        for hh in range(HEADS):
            sfin_ref[0, hh] = st_scr[hh].T


def _hgrn_prompt(qs, lf, kk, ib, gs, gout, s0t, lev, *, layer, batch, seq_len):
    m = qs.shape[0]
    rows = HGRN_STEP_CHUNKS * CHUNK
    assert seq_len % rows == 0
    nc = seq_len // rows
    row_spec = pl.BlockSpec((rows, SEG), lambda b, c: (b * nc + c, 0))
    state_spec = pl.BlockSpec((1, HEADS, HEAD_DIM, HEAD_DIM), lambda b, c: (b, 0, 0, 0))
    return pl.pallas_call(
        _hgrn_prompt_kernel,
        grid=(batch, nc),
        in_specs=[row_spec, row_spec, row_spec, row_spec, row_spec,
                  _layer_vec_spec(layer, SEG, 2),
                  state_spec,
                  pl.BlockSpec((CHUNK, CHUNK), lambda b, c: (0, 0))],
        out_specs=[row_spec, state_spec],
        out_shape=[jax.ShapeDtypeStruct((m, SEG), BF16),
                   jax.ShapeDtypeStruct((batch, HEADS, HEAD_DIM, HEAD_DIM), F32)],
        scratch_shapes=[pltpu.VMEM((HEADS, HEAD_DIM, HEAD_DIM), F32)],
        compiler_params=pltpu.CompilerParams(
            dimension_semantics=("arbitrary", "arbitrary"), vmem_limit_bytes=VMEM_LIMIT),
        name="hgrn_prompt",
    )(qs, lf, kk, ib, gs, gout, s0t, lev)


def _hgrn_decode_kernel(lft_ref, kkt_ref, qst_ref, ib_ref, gs_ref, gout_ref, s0_ref, *rest):
    ob_ref, snew_ref, o_scr = rest[-3:]
    tb = s0_ref.shape[0]
    for bb in range(tb):
        for hh in range(HEADS):
            ks = slice(hh * HEAD_DIM, (hh + 1) * HEAD_DIM)
            f_col = jnp.exp(lft_ref[0, ks, bb:bb + 1])
            k_col = kkt_ref[0, ks, bb:bb + 1].astype(F32)
            q_col = qst_ref[0, ks, bb:bb + 1].astype(F32)
            i_row = ib_ref[bb:bb + 1, ks].astype(F32)
            s_new = f_col * s0_ref[bb, hh] + k_col * i_row
            snew_ref[bb, hh] = s_new
            o_scr[bb:bb + 1, ks] = jnp.sum(q_col * s_new, axis=0, keepdims=True)
    for hh in range(HEADS):
        ks = slice(hh * HEAD_DIM, (hh + 1) * HEAD_DIM)
        o = o_scr[:, ks]
        on = o * lax.rsqrt(jnp.mean(o * o, axis=-1, keepdims=True) + EPS)
        ob_ref[:, ks] = (on * gout_ref[:, ks] * gs_ref[:, ks].astype(F32)).astype(BF16)


def _hgrn_decode(lft, kkt, qst, ib, gs, gout, s0_all, s_new_all, *, layer, tb):
    m = ib.shape[0]
    col_spec = pl.BlockSpec((1, SEG, tb), lambda i: (i, 0, 0))
    row_spec = pl.BlockSpec((tb, SEG), lambda i: (i, 0))
    state_spec = pl.BlockSpec((None, tb, HEADS, HEAD_DIM, HEAD_DIM), lambda i: (layer, i, 0, 0, 0))
    in_specs = [col_spec, col_spec, col_spec, row_spec, row_spec,
                _layer_vec_spec(layer, SEG, 1), state_spec]
    args = [lft, kkt, qst, ib, gs, gout, s0_all]
    aliases = {}
    if s_new_all is not None:
        in_specs.append(pl.BlockSpec(memory_space=pl.ANY))
        args.append(s_new_all)
        aliases = {len(args) - 1: 1}
    return pl.pallas_call(
        _hgrn_decode_kernel,
        grid=(m // tb,),
        in_specs=in_specs,
        out_specs=[row_spec, state_spec],
        out_shape=[jax.ShapeDtypeStruct((m, SEG), BF16),
                   jax.ShapeDtypeStruct(s0_all.shape, F32)],
        input_output_aliases=aliases,
        scratch_shapes=[pltpu.VMEM((tb, SEG), F32)],
        compiler_params=pltpu.CompilerParams(
            dimension_semantics=("arbitrary",), vmem_limit_bytes=VMEM_LIMIT),
        name="hgrn_decode",
    )(*args)


def _outproj_kernel(oa_ref, ob_ref, w_ref, x_ref, g_ref, o_ref):
    mix = _dot(oa_ref[...], w_ref[:SEG, :]) + _dot(ob_ref[...], w_ref[SEG:, :])
    o_ref[...] = x_ref[...] + _rms_norm(mix, g_ref[...])


def _outproj_cast_kernel(oa_ref, ob_ref, w_ref, x_ref, g_ref, o_ref, wb_ref, acc_scr):
    k = pl.program_id(0)
    wb = w_ref[...].astype(BF16)
    wb_ref[...] = wb

    @pl.when(k == 0)
    def _():
        acc_scr[...] = _dot(oa_ref[...], wb)

    @pl.when(k == 1)
    def _():
        o_ref[...] = x_ref[...] + _rms_norm(acc_scr[...] + _dot(ob_ref[...], wb), g_ref[...])


def _outproj_cast(oa, ob, w, x, g, *, layer):
    m, d = x.shape
    whole = lambda shape: pl.BlockSpec(shape, lambda k: (0,) * len(shape))
    return pl.pallas_call(
        _outproj_cast_kernel,
        grid=(2,),
        in_specs=[whole((m, SEG)), whole((m, SEG)),
                  pl.BlockSpec((None, SEG, d), lambda k: (layer, k, 0)),
                  whole((m, d)),
                  _layer_vec_spec(layer, d, 1)],
        out_specs=[whole((m, d)), pl.BlockSpec((SEG, d), lambda k: (k, 0))],
        out_shape=[jax.ShapeDtypeStruct((m, d), F32),
                   jax.ShapeDtypeStruct((2 * SEG, d), BF16)],
        scratch_shapes=[pltpu.VMEM((m, d), F32)],
        compiler_params=pltpu.CompilerParams(
            dimension_semantics=("arbitrary",), vmem_limit_bytes=VMEM_LIMIT),
        name="outproj_decode",
    )(oa, ob, w, x, g)


def _outproj(oa, ob, w, x, g, *, layer, tm):
    m, d = x.shape
    return pl.pallas_call(
        _outproj_kernel,
        grid=(m // tm,),
        in_specs=[pl.BlockSpec((tm, SEG), lambda i: (i, 0)),
                  pl.BlockSpec((tm, SEG), lambda i: (i, 0)),
                  pl.BlockSpec(w.shape, lambda i: (0, 0)),
                  pl.BlockSpec((tm, d), lambda i: (i, 0)),
                  _layer_vec_spec(layer, d, 1)],
        out_specs=pl.BlockSpec((tm, d), lambda i: (i, 0)),
        out_shape=jax.ShapeDtypeStruct((m, d), F32),
        compiler_params=pltpu.CompilerParams(
            dimension_semantics=("arbitrary",), vmem_limit_bytes=VMEM_LIMIT),
        name="outproj",
    )(oa, ob, w, x, g)


def _ffn_kernel(x_ref, gpre_ref, wg_ref, wu_ref, wd_ref, gpost_ref, o_ref, *rest, cast):
    h_scr = rest[-1]
    f = pl.program_id(1)
    nf = pl.num_programs(1)
    tm = x_ref.shape[0]
    if cast:
        for src, dst in zip((wg_ref, wu_ref, wd_ref), rest[:3]):
            dst[...] = src[...].astype(BF16)
        wg_ref, wu_ref, wd_ref = rest[:3]

    def contribution(rows):
        h = h_scr[rows, :]
        gate = _dot(h, wg_ref[...])
        up = _dot(h, wu_ref[...])
        t = (gate * _sigmoid(gate) * up).astype(BF16)
        return _dot(t, wd_ref[...])

    @pl.when(f == 0)
    def _():
        rc = min(tm, ROW_CHUNK)
        for r in range(tm // rc):
            rows = slice(r * rc, (r + 1) * rc)
            h_scr[rows, :] = _rms_norm(x_ref[rows, :], gpre_ref[...]).astype(BF16)
        o_ref[...] = contribution(slice(None))

    @pl.when(jnp.logical_and(f > 0, f < nf - 1))
    def _():
        o_ref[...] += contribution(slice(None))

    @pl.when(f == nf - 1)
    def _():
        rb = min(tm, FFN_LAST_ROW_BLOCK)
        for r in range(tm // rb):
            rows = slice(r * rb, (r + 1) * rb)
            acc = o_ref[rows, :] + contribution(rows)
            o_ref[rows, :] = x_ref[rows, :] + _rms_norm(acc, gpost_ref[...])


def _ffn(x, gpre, wg, wu, wd, gpost, *, layer, tm, tf, cast):
    m, d = x.shape
    dff = wg.shape[-1]
    assert dff // tf >= 2
    col_spec = pl.BlockSpec((d, tf), lambda i, f: (0, f))
    row_spec = pl.BlockSpec((tf, d), lambda i, f: (f, 0))
    out_specs = [pl.BlockSpec((tm, d), lambda i, f: (i, 0))]
    out_shape = [jax.ShapeDtypeStruct((m, d), F32)]
    if cast:
        assert m == tm
        w_specs = [pl.BlockSpec((None, d, tf), lambda i, f: (layer, 0, f)),
                   pl.BlockSpec((None, d, tf), lambda i, f: (layer, 0, f)),
                   pl.BlockSpec((None, tf, d), lambda i, f: (layer, f, 0))]
        out_specs += [col_spec, col_spec, row_spec]
        out_shape += [jax.ShapeDtypeStruct((d, dff), BF16), jax.ShapeDtypeStruct((d, dff), BF16),
                      jax.ShapeDtypeStruct((dff, d), BF16)]
    else:
        w_specs = [col_spec, col_spec, row_spec]
    res = pl.pallas_call(
        functools.partial(_ffn_kernel, cast=cast),
        grid=(m // tm, dff // tf),
        in_specs=[pl.BlockSpec((tm, d), lambda i, f: (i, 0)),
                  _layer_vec_spec(layer, d, 2),
                  *w_specs,
                  _layer_vec_spec(layer, d, 2)],
        out_specs=out_specs,
        out_shape=out_shape,
        scratch_shapes=[pltpu.VMEM((tm, d), BF16)],
        compiler_params=pltpu.CompilerParams(
            dimension_semantics=("arbitrary", "arbitrary"), vmem_limit_bytes=VMEM_LIMIT),
        name="ffn_decode" if cast else "ffn",
    )(x, gpre, wg, wu, wd, gpost)
    return res if cast else res[0]


PROMPT_TM = 1024
OUTPROJ_TM = 512
FFN_TF = 512
DECODE_TB = 16


def kernel(x_prompt, x_sample, state_hgrn, norm_mix_pre, norm_mix_post, w_in, ln_v_gain, ln_v_bias,
           spatial_w, spatial_b, lb_param, hgrn_out_gain, w_out, norm_ffn_pre, norm_ffn_post,
           w_gate, w_up, w_down):
    depth = w_in.shape[0]
    batch, seq_len, d = x_prompt.shape
    dec_batch, dec_seq, _ = x_sample.shape
    assert dec_seq == 1
    assert dec_batch % DECODE_TB == 0 and w_in.shape[2] == 6 * SEG and w_out.shape[1] == 2 * SEG

    lev = jnp.asarray(_level_table())
    vec = lambda a: a.reshape(depth, 1, -1)
    g_mix_pre, g_mix_post = vec(norm_mix_pre), vec(norm_mix_post)
    g_ffn_pre, g_ffn_post = vec(norm_ffn_pre), vec(norm_ffn_post)
    ln_g, ln_b, g_out = vec(ln_v_gain), vec(ln_v_bias), vec(hgrn_out_gain)
    spatial_bt = spatial_b.transpose(0, 2, 1)
    w00 = vec(jnp.repeat(spatial_w[:, :, 0, 0], HEAD_DIM, axis=1))
    b0 = vec(jnp.repeat(spatial_b[:, :, 0], HEAD_DIM, axis=1))

    xp = x_prompt.reshape(batch * seq_len, d)
    xs = x_sample.reshape(dec_batch, d)
    s0t_zero = jnp.zeros((batch, HEADS, HEAD_DIM, HEAD_DIM), F32)
    sp_out, vp_out, vs_out = [], [], []
    ss_all = None

    for l in range(depth):
        oa, vrows, hs, w_in_a = _inproj_a(
            xs, g_mix_pre, w_in, ln_g, ln_b, w00, b0, layer=l, decode=True, tm=dec_batch, seq_len=1)
        qs, lf, kk, ib, gs, w_in_b = _inproj_b(hs, w_in, lb_param, layer=l, decode=True, tm=dec_batch)
        cols = lambda a: a.reshape(dec_batch // DECODE_TB, DECODE_TB, SEG).transpose(0, 2, 1)
        ob, ss_all = _hgrn_decode(cols(lf), cols(kk), cols(qs), ib, gs, g_out, state_hgrn, ss_all,
                                  layer=l, tb=DECODE_TB)
        xs, w_out_b = _outproj_cast(oa, ob, w_out, xs, g_mix_post, layer=l)
        xs, w_gate_b, w_up_b, w_down_b = _ffn(xs, g_ffn_pre, w_gate, w_up, w_down, g_ffn_post,
                                              layer=l, tm=dec_batch, tf=FFN_TF, cast=True)
        vs_out.append(vrows.reshape(dec_batch, 1, SEG))
        oa, vrows, hp = _inproj_a(
            xp, g_mix_pre, w_in_a, ln_g, ln_b, spatial_w, spatial_bt,
            layer=l, decode=False, tm=PROMPT_TM, seq_len=seq_len)
        qs, lf, kk, ib, gs = _inproj_b(hp, w_in_b, lb_param, layer=l, decode=False, tm=PROMPT_TM)
        ob, s_fin = _hgrn_prompt(qs, lf, kk, ib, gs, g_out, s0t_zero, lev,
                                 layer=l, batch=batch, seq_len=seq_len)
        xp = _outproj(oa, ob, w_out_b, xp, g_mix_post, layer=l, tm=OUTPROJ_TM)
        xp = _ffn(xp, g_ffn_pre, w_gate_b, w_up_b, w_down_b, g_ffn_post,
                  layer=l, tm=PROMPT_TM, tf=FFN_TF, cast=False)
        sp_out.append(s_fin)
        vp_out.append(vrows.reshape(batch, CHUNK, SEG))

    return (xp.reshape(batch, seq_len, d), xs.reshape(dec_batch, 1, d),
            jnp.stack(sp_out), ss_all, jnp.stack(vp_out), jnp.stack(vs_out))
```

```python
import functools

import numpy as np
import jax
import jax.numpy as jnp
from jax import lax
from jax.experimental import pallas as pl
from jax.experimental.pallas import tpu as pltpu

F32 = jnp.float32
BF16 = jnp.bfloat16

EPS = 1e-6
LANES = 128
SUBLANES = 8
SEG = 1024
HEADS = 8
HEAD_DIM = SEG // HEADS
CHUNK = 128
N_LEVELS = 7
HGRN_STEP_CHUNKS = 4
HEAD_GROUP = 4
ROW_CHUNK = 256
COL_CHUNK = 256
V_ROW_BLOCK = 512
FFN_LAST_ROW_BLOCK = 512
LOG2E = np.float32(1.4426950408889634)
VMEM_LIMIT = 60 * 1024 * 1024


def _rms_norm(x, g):
    return x * lax.rsqrt(jnp.mean(x * x, axis=-1, keepdims=True) + EPS) * g


def _sigmoid(x):
    return 1.0 / (1.0 + jnp.exp(-x))


def _gelu(x):
    return 0.5 * x * (1.0 + lax.erf(x * np.float32(np.sqrt(0.5))))


def _dot(a, b):
    return jnp.dot(a, b, preferred_element_type=F32)


def _dot_nt(a, b):
    return lax.dot_general(a, b, (((1,), (1,)), ((), ())), preferred_element_type=F32)


def _dot_tn(a, b):
    return lax.dot_general(a, b, (((0,), (0,)), ((), ())), preferred_element_type=F32)


def _forget_lower_bound(lbp, layer):
    m = jnp.max(lbp, axis=0, keepdims=True)
    e = jnp.exp(lbp - m)
    sm = e / jnp.sum(e, axis=0, keepdims=True)
    cum = sm[0:1]
    for r in range(1, layer + 1):
        cum = cum + sm[r:r + 1]
    return cum - sm[0:1]


def _layer_vec_spec(layer, width, grid_rank):
    if grid_rank == 1:
        return pl.BlockSpec((None, 1, width), lambda i: (layer, 0, 0))
    return pl.BlockSpec((None, 1, width), lambda i, j: (layer, 0, 0))


def _by_columns(h_ref, w_ref, epilogue):
    for c in range(SEG // COL_CHUNK):
        cols = slice(c * COL_CHUNK, (c + 1) * COL_CHUNK)
        epilogue(_dot(h_ref[...], w_ref[:, cols]), cols)


def _inproj_a_kernel(x_ref, gpre_ref, w_ref, lng_ref, lnb_ref, ws_ref, bs_ref,
                     oa_ref, vrows_ref, h_ref, *rest, decode):
    wb_ref = rest[0] if decode else None
    u_scr = rest[-1]
    j = pl.program_id(1)
    tm = x_ref.shape[0]
    if decode:
        wb_ref[...] = w_ref[...].astype(BF16)
    wmat_ref = wb_ref if decode else w_ref

    @pl.when(j == 0)
    def _():
        rc = min(tm, ROW_CHUNK)
        for r in range(tm // rc):
            rows = slice(r * rc, (r + 1) * rc)
            h_ref[rows, :] = _rms_norm(x_ref[rows, :], gpre_ref[...]).astype(BF16)

        def epilogue(acc, cols):
            u_scr[:, cols] = _gelu(acc)
        _by_columns(h_ref, wmat_ref, epilogue)

    @pl.when(j == 1)
    def _():
        if not decode:
            row = lax.broadcasted_iota(jnp.int32, (CHUNK, CHUNK), 0)
            col = lax.broadcasted_iota(jnp.int32, (CHUNK, CHUNK), 1)
            causal = row >= col
        rb = min(tm, V_ROW_BLOCK)
        for r in range(tm // rb):
            rows = slice(r * rb, (r + 1) * rb)
            gv = _gelu(_dot(h_ref[rows, :], wmat_ref[...]))
            mu = jnp.mean(gv, axis=-1, keepdims=True)
            vc = gv - mu
            v = vc * lax.rsqrt(jnp.mean(vc * vc, axis=-1, keepdims=True) + EPS)
            v = v * lng_ref[...] + lnb_ref[...]
            if decode:
                vrows_ref[rows, :] = v
                oa_ref[rows, :] = (u_scr[rows, :] * (v * ws_ref[...] + bs_ref[...])).astype(BF16)
                continue
            if r == tm // rb - 1:
                vrows_ref[...] = v[rb - CHUNK:, :]
            vb = v.astype(BF16)
            for hh in range(HEADS):
                cs = slice(hh * HEAD_DIM, (hh + 1) * HEAD_DIM)
                w_h = jnp.where(causal, ws_ref[hh], 0.0).astype(BF16)
                b_h = bs_ref[:, hh:hh + 1]
                for c in range(rb // CHUNK):
                    s = _dot(w_h, vb[c * CHUNK:(c + 1) * CHUNK, cs]) + b_h
                    rs = slice(rows.start + c * CHUNK, rows.start + (c + 1) * CHUNK)
                    oa_ref[rs, cs] = (u_scr[rs, cs] * s).astype(BF16)


def _inproj_a(x, gpre, w, lng, lnb, ws, bs, *, layer, decode, tm, seq_len):
    m, d = x.shape
    if decode:
        assert m == tm
        vrows_shape, vrows_spec = (m, SEG), pl.BlockSpec((tm, SEG), lambda i, j: (i, 0))
        ws_spec = _layer_vec_spec(layer, SEG, 2)
        bs_spec = _layer_vec_spec(layer, SEG, 2)
        w_spec = pl.BlockSpec((None, d, SEG), lambda i, j: (layer, 0, j))
    else:
        assert seq_len % tm == 0 and tm % CHUNK == 0
        tiles_per_seq = seq_len // tm
        vrows_shape = (m // seq_len * CHUNK, SEG)
        vrows_spec = pl.BlockSpec((CHUNK, SEG), lambda i, j: (i // tiles_per_seq, 0))
        ws_spec = pl.BlockSpec((None, HEADS, CHUNK, CHUNK), lambda i, j: (layer, 0, 0, 0))
        bs_spec = pl.BlockSpec((None, CHUNK, HEADS), lambda i, j: (layer, 0, 0))
        w_spec = pl.BlockSpec((d, SEG), lambda i, j: (0, j))
    out_shape = [jax.ShapeDtypeStruct((m, SEG), BF16),
                 jax.ShapeDtypeStruct(vrows_shape, F32),
                 jax.ShapeDtypeStruct((m, d), BF16)]
    out_specs = [pl.BlockSpec((tm, SEG), lambda i, j: (i, 0)), vrows_spec,
                 pl.BlockSpec((tm, d), lambda i, j: (i, 0))]
    if decode:
        out_shape.append(jax.ShapeDtypeStruct((d, 2 * SEG), BF16))
        out_specs.append(pl.BlockSpec((d, SEG), lambda i, j: (0, j)))
    return pl.pallas_call(
        functools.partial(_inproj_a_kernel, decode=decode),
        grid=(m // tm, 2),
        in_specs=[pl.BlockSpec((tm, d), lambda i, j: (i, 0)),
                  _layer_vec_spec(layer, d, 2),
                  w_spec,
                  _layer_vec_spec(layer, SEG, 2), _layer_vec_spec(layer, SEG, 2), ws_spec, bs_spec],
        out_specs=out_specs,
        out_shape=out_shape,
        scratch_shapes=[pltpu.VMEM((tm, SEG), F32)],
        compiler_params=pltpu.CompilerParams(
            dimension_semantics=("arbitrary", "arbitrary"), vmem_limit_bytes=VMEM_LIMIT),
        name="inproj_a_decode" if decode else "inproj_a",
    )(x, gpre, w, lng, lnb, ws, bs)


def _inproj_b_kernel(h_ref, w_ref, lbp_ref, qs_ref, lf_ref, kk_ref, ib_ref, gs_ref, *rest,
                     layer, decode):
    j = pl.program_id(1)
    if decode:
        wb_ref = rest[0]
        wb_ref[...] = w_ref[...].astype(BF16)
        w_ref = wb_ref

    @pl.when(j == 0)
    def _():
        def epilogue(acc, cols):
            qs_ref[:, cols] = (acc * _sigmoid(acc)).astype(BF16)
        _by_columns(h_ref, w_ref, epilogue)

    @pl.when(j == 1)
    def _():
        lb_all = _forget_lower_bound(lbp_ref[...], layer)

        def epilogue(z, cols):
            ez = jnp.exp(-jnp.abs(z))
            one_plus = 1.0 + ez
            inv = 1.0 / one_plus
            logsig = jnp.minimum(z, 0.0) - jnp.log(one_plus)
            sig_neg = jnp.where(z >= 0.0, ez, 1.0) * inv
            if layer == 0:
                lf_ref[:, cols] = logsig
                kk_ref[:, cols] = sig_neg.astype(BF16)
            else:
                lb = lb_all[:, cols]
                f = lb + (1.0 - lb) * (jnp.where(z >= 0.0, 1.0, ez) * inv)
                lf_ref[:, cols] = jnp.maximum(jnp.log(f), logsig)
                kk_ref[:, cols] = ((1.0 - lb) * sig_neg).astype(BF16)
        _by_columns(h_ref, w_ref, epilogue)

    @pl.when(j == 2)
    def _():
        def epilogue(acc, cols):
            ib_ref[:, cols] = acc.astype(BF16)
        _by_columns(h_ref, w_ref, epilogue)

    @pl.when(j == 3)
    def _():
        def epilogue(acc, cols):
            gs_ref[:, cols] = (acc * _sigmoid(acc)).astype(BF16)
        _by_columns(h_ref, w_ref, epilogue)


def _inproj_b(h, w, lbp, *, layer, decode, tm):
    m, d = h.shape
    n_seg = 4
    if decode:
        assert m == tm
        w_spec = pl.BlockSpec((None, d, SEG), lambda i, j: (layer, 0, j + 2))
    else:
        w_spec = pl.BlockSpec((d, SEG), lambda i, j: (0, j))
    row_spec = pl.BlockSpec((tm, SEG), lambda i, j: (i, 0))
    out_shape = [jax.ShapeDtypeStruct((m, SEG), BF16),
                 jax.ShapeDtypeStruct((m, SEG), F32),
                 jax.ShapeDtypeStruct((m, SEG), BF16),
                 jax.ShapeDtypeStruct((m, SEG), BF16),
                 jax.ShapeDtypeStruct((m, SEG), BF16)]
    out_specs = [row_spec] * 5
    if decode:
        out_shape.append(jax.ShapeDtypeStruct((d, n_seg * SEG), BF16))
        out_specs.append(pl.BlockSpec((d, SEG), lambda i, j: (0, j)))
    return pl.pallas_call(
        functools.partial(_inproj_b_kernel, layer=layer, decode=decode),
        grid=(m // tm, n_seg),
        in_specs=[pl.BlockSpec((tm, d), lambda i, j: (i, 0)),
                  w_spec,
                  pl.BlockSpec(lbp.shape, lambda i, j: (0, 0))],
        out_specs=out_specs,
        out_shape=out_shape,
        compiler_params=pltpu.CompilerParams(
            dimension_semantics=("arbitrary", "arbitrary"), vmem_limit_bytes=VMEM_LIMIT),
        name="inproj_b_decode" if decode else "inproj_b",
    )(h, w, lbp)


def _level_table():
    t = np.arange(CHUNK)[:, None]
    s = np.arange(CHUNK)[None, :]
    x = t ^ s
    lev = np.floor(np.log2(np.maximum(x, 1))).astype(np.int32)
    lev = np.where(t == s, -1, lev)
    lev = np.where(t < s, -2, lev)
    return lev.astype(np.int32)


def _log_gap_to_block_boundary(b, level):
    half = 1 << level
    blk = 2 * half
    k = b.shape[1]
    if half >= SUBLANES:
        bb = b.reshape(CHUNK // blk, blk, k)
        r = bb[:, half - 1:half, :]
        gap = jnp.concatenate([r - bb[:, :half, :], bb[:, half:, :] - r], axis=1)
        return gap.reshape(CHUNK, k)
    b3 = b.reshape(CHUNK // SUBLANES, SUBLANES, k)
    sub = lax.broadcasted_iota(jnp.int32, (1, SUBLANES, k), 1)
    r = None
    for start in range(0, SUBLANES, blk):
        cand = jnp.broadcast_to(b3[:, start + half - 1:start + half, :], b3.shape)
        r = cand if r is None else jnp.where(sub >= start, cand, r)
    sign = jnp.where((sub & half) != 0, 1.0, -1.0)
    return ((b3 - r) * sign).reshape(CHUNK, k)


def _second_half_rows_from(a, b, level):
    half = 1 << level
    blk = 2 * half
    k = a.shape[1]
    if half >= SUBLANES:
        a3 = a.reshape(CHUNK // blk, blk, k)
        b3 = b.reshape(CHUNK // blk, blk, k)
        return jnp.concatenate([b3[:, :half, :], a3[:, half:, :]], axis=1).reshape(CHUNK, k)
    sub = lax.broadcasted_iota(jnp.int32, (1, SUBLANES, k), 1)
    a3 = a.reshape(CHUNK // SUBLANES, SUBLANES, k)
    b3 = b.reshape(CHUNK // SUBLANES, SUBLANES, k)
    return jnp.where((sub & half) != 0, a3, b3).reshape(CHUNK, k)


def _hgrn_chunk(qs_ref, lf_ref, kk_ref, ib_ref, gs_ref, gout_ref, ob_ref, st_scr, lev, tri, first):
    odd_row = (lax.broadcasted_iota(jnp.int32, (CHUNK, HEAD_DIM), 0) & 1) == 1
    heads = [slice(hh * HEAD_DIM, (hh + 1) * HEAD_DIM) for hh in range(first, first + HEAD_GROUP)]
    lf2, q, kk, b, o, a = [], [], [], [], [], []
    for hh, cs in enumerate(heads):
        lf2.append(lf_ref[:, cs] * LOG2E)
        hi = lf2[hh].astype(BF16)
        r1 = lf2[hh] - hi.astype(F32)
        mid = r1.astype(BF16)
        lo = (r1 - mid.astype(F32)).astype(BF16)
        b.append(_dot(tri, hi) + _dot(tri, mid) + _dot(tri, lo))
    for hh, cs in enumerate(heads):
        qb = qs_ref[:, cs]
        kb = kk_ref[:, cs]
        q.append(qb.astype(F32))
        kk.append(kb.astype(F32))
        o.append(_dot_nt((q[hh] * jnp.exp2(b[hh])).astype(BF16), st_scr[first + hh].astype(BF16)))
        a.append(jnp.where(lev == -1, _dot_nt(qb, kb), 0.0))

    for level in range(N_LEVELS):
        for hh in range(HEAD_GROUP):
            if level == 0:
                d = jnp.where(odd_row, lf2[hh], 0.0)
            else:
                d = _log_gap_to_block_boundary(b[hh], level)
            x = (_second_half_rows_from(q[hh], kk[hh], level) * jnp.exp2(d)).astype(BF16)
            a[hh] = jnp.where(lev == level, _dot_nt(x, x), a[hh])

    for hh, cs in enumerate(heads):
        o[hh] = o[hh] + _dot(a[hh].astype(BF16), ib_ref[:, cs])
    for hh, cs in enumerate(heads):
        b_last = b[hh][CHUNK - 1:CHUNK, :]
        kd = (kk[hh] * jnp.exp2(b_last - b[hh])).astype(BF16)
        st_scr[first + hh] = st_scr[first + hh] * jnp.exp2(b_last) + _dot_tn(ib_ref[:, cs], kd)
    for hh, cs in enumerate(heads):
        on = o[hh] * lax.rsqrt(jnp.mean(o[hh] * o[hh], axis=-1, keepdims=True) + EPS)
        ob_ref[:, cs] = (on * gout_ref[:, cs] * gs_ref[:, cs].astype(F32)).astype(BF16)


def _hgrn_prompt_kernel(qs_ref, lf_ref, kk_ref, ib_ref, gs_ref, gout_ref, s0t_ref, lev_ref,
                        ob_ref, sfin_ref, st_scr):
    c = pl.program_id(1)

    @pl.when(c == 0)
    def _():
        st_scr[...] = s0t_ref[0]

    lev = lev_ref[...]
    row = lax.broadcasted_iota(jnp.int32, (CHUNK, CHUNK), 0)
    col = lax.broadcasted_iota(jnp.int32, (CHUNK, CHUNK), 1)
    tri = jnp.where(row >= col, 1.0, 0.0).astype(BF16)

    for ci in range(lf_ref.shape[0] // CHUNK):
        rows = pl.ds(ci * CHUNK, CHUNK)
        for first in range(0, HEADS, HEAD_GROUP):
            _hgrn_chunk(qs_ref.at[rows], lf_ref.at[rows], kk_ref.at[rows], ib_ref.at[rows],
                        gs_ref.at[rows], gout_ref, ob_ref.at[rows], st_scr, lev, tri, first)

    @pl.when(c == pl.num_programs(1) - 1)
    def _():
        for hh in range(HEADS):
            sfin_ref[0, hh] = st_scr[hh].T


def _hgrn_prompt(qs, lf, kk, ib, gs, gout, s0t, lev, *, layer, batch, seq_len):
    m = qs.shape[0]
    rows = HGRN_STEP_CHUNKS * CHUNK
    assert seq_len % rows == 0
    nc = seq_len // rows
    row_spec = pl.BlockSpec((rows, SEG), lambda b, c: (b * nc + c, 0))
    state_spec = pl.BlockSpec((1, HEADS, HEAD_DIM, HEAD_DIM), lambda b, c: (b, 0, 0, 0))
    return pl.pallas_call(
        _hgrn_prompt_kernel,
        grid=(batch, nc),
        in_specs=[row_spec, row_spec, row_spec, row_spec, row_spec,
                  _layer_vec_spec(layer, SEG, 2),
                  state_spec,
                  pl.BlockSpec((CHUNK, CHUNK), lambda b, c: (0, 0))],
        out_specs=[row_spec, state_spec],
        out_shape=[jax.ShapeDtypeStruct((m, SEG), BF16),
                   jax.ShapeDtypeStruct((batch, HEADS, HEAD_DIM, HEAD_DIM), F32)],
        scratch_shapes=[pltpu.VMEM((HEADS, HEAD_DIM, HEAD_DIM), F32)],
        compiler_params=pltpu.CompilerParams(
            dimension_semantics=("arbitrary", "arbitrary"), vmem_limit_bytes=VMEM_LIMIT),
        name="hgrn_prompt",
    )(qs, lf, kk, ib, gs, gout, s0t, lev)


def _hgrn_decode_kernel(lft_ref, kkt_ref, qst_ref, ib_ref, gs_ref, gout_ref, s0_ref, *rest):
    ob_ref, snew_ref, o_scr = rest[-3:]
    tb = s0_ref.shape[0]
    for bb in range(tb):
        for hh in range(HEADS):
            ks = slice(hh * HEAD_DIM, (hh + 1) * HEAD_DIM)
            f_col = jnp.exp(lft_ref[0, ks, bb:bb + 1])
            k_col = kkt_ref[0, ks, bb:bb + 1].astype(F32)
            q_col = qst_ref[0, ks, bb:bb + 1].astype(F32)
            i_row = ib_ref[bb:bb + 1, ks].astype(F32)
            s_new = f_col * s0_ref[bb, hh] + k_col * i_row
            snew_ref[bb, hh] = s_new
            o_scr[bb:bb + 1, ks] = jnp.sum(q_col * s_new, axis=0, keepdims=True)
    for hh in range(HEADS):
        ks = slice(hh * HEAD_DIM, (hh + 1) * HEAD_DIM)
        o = o_scr[:, ks]
        on = o * lax.rsqrt(jnp.mean(o * o, axis=-1, keepdims=True) + EPS)
        ob_ref[:, ks] = (on * gout_ref[:, ks] * gs_ref[:, ks].astype(F32)).astype(BF16)


def _hgrn_decode(lft, kkt, qst, ib, gs, gout, s0_all, s_new_all, *, layer, tb):
    m = ib.shape[0]
    col_spec = pl.BlockSpec((1, SEG, tb), lambda i: (i, 0, 0))
    row_spec = pl.BlockSpec((tb, SEG), lambda i: (i, 0))
    state_spec = pl.BlockSpec((None, tb, HEADS, HEAD_DIM, HEAD_DIM), lambda i: (layer, i, 0, 0, 0))
    in_specs = [col_spec, col_spec, col_spec, row_spec, row_spec,
                _layer_vec_spec(layer, SEG, 1), state_spec]
    args = [lft, kkt, qst, ib, gs, gout, s0_all]
    aliases = {}
    if s_new_all is not None:
        in_specs.append(pl.BlockSpec(memory_space=pl.ANY))
        args.append(s_new_all)
        aliases = {len(args) - 1: 1}
    return pl.pallas_call(
        _hgrn_decode_kernel,
        grid=(m // tb,),
        in_specs=in_specs,
        out_specs=[row_spec, state_spec],
        out_shape=[jax.ShapeDtypeStruct((m, SEG), BF16),
                   jax.ShapeDtypeStruct(s0_all.shape, F32)],
        input_output_aliases=aliases,
        scratch_shapes=[pltpu.VMEM((tb, SEG), F32)],
        compiler_params=pltpu.CompilerParams(
            dimension_semantics=("arbitrary",), vmem_limit_bytes=VMEM_LIMIT),
        name="hgrn_decode",
    )(*args)


def _outproj_kernel(oa_ref, ob_ref, w_ref, x_ref, g_ref, o_ref):
    mix = _dot(oa_ref[...], w_ref[:SEG, :]) + _dot(ob_ref[...], w_ref[SEG:, :])
    o_ref[...] = x_ref[...] + _rms_norm(mix, g_ref[...])


def _outproj_cast_kernel(oa_ref, ob_ref, w_ref, x_ref, g_ref, o_ref, wb_ref, acc_scr):
    k = pl.program_id(0)
    wb = w_ref[...].astype(BF16)
    wb_ref[...] = wb

    @pl.when(k == 0)
    def _():
        acc_scr[...] = _dot(oa_ref[...], wb)

    @pl.when(k == 1)
    def _():
        o_ref[...] = x_ref[...] + _rms_norm(acc_scr[...] + _dot(ob_ref[...], wb), g_ref[...])


def _outproj_cast(oa, ob, w, x, g, *, layer):
    m, d = x.shape
    whole = lambda shape: pl.BlockSpec(shape, lambda k: (0,) * len(shape))
    return pl.pallas_call(
        _outproj_cast_kernel,
        grid=(2,),
        in_specs=[whole((m, SEG)), whole((m, SEG)),
                  pl.BlockSpec((None, SEG, d), lambda k: (layer, k, 0)),
                  whole((m, d)),
                  _layer_vec_spec(layer, d, 1)],
        out_specs=[whole((m, d)), pl.BlockSpec((SEG, d), lambda k: (k, 0))],
        out_shape=[jax.ShapeDtypeStruct((m, d), F32),
                   jax.ShapeDtypeStruct((2 * SEG, d), BF16)],
        scratch_shapes=[pltpu.VMEM((m, d), F32)],
        compiler_params=pltpu.CompilerParams(
            dimension_semantics=("arbitrary",), vmem_limit_bytes=VMEM_LIMIT),
        name="outproj_decode",
    )(oa, ob, w, x, g)


def _outproj(oa, ob, w, x, g, *, layer, tm):
    m, d = x.shape
    return pl.pallas_call(
        _outproj_kernel,
        grid=(m // tm,),
        in_specs=[pl.BlockSpec((tm, SEG), lambda i: (i, 0)),
                  pl.BlockSpec((tm, SEG), lambda i: (i, 0)),
                  pl.BlockSpec(w.shape, lambda i: (0, 0)),
                  pl.BlockSpec((tm, d), lambda i: (i, 0)),
                  _layer_vec_spec(layer, d, 1)],
        out_specs=pl.BlockSpec((tm, d), lambda i: (i, 0)),
        out_shape=jax.ShapeDtypeStruct((m, d), F32),
        compiler_params=pltpu.CompilerParams(
            dimension_semantics=("arbitrary",), vmem_limit_bytes=VMEM_LIMIT),
        name="outproj",
    )(oa, ob, w, x, g)


def _ffn_kernel(x_ref, gpre_ref, wg_ref, wu_ref, wd_ref, gpost_ref, o_ref, *rest, cast):
    h_scr = rest[-1]
    f = pl.program_id(1)
    nf = pl.num_programs(1)
    tm = x_ref.shape[0]
    if cast:
        for src, dst in zip((wg_ref, wu_ref, wd_ref), rest[:3]):
            dst[...] = src[...].astype(BF16)
        wg_ref, wu_ref, wd_ref = rest[:3]

    def contribution(rows):
        h = h_scr[rows, :]
        gate = _dot(h, wg_ref[...])
        up = _dot(h, wu_ref[...])
        t = (gate * _sigmoid(gate) * up).astype(BF16)
        return _dot(t, wd_ref[...])

    @pl.when(f == 0)
    def _():
        rc = min(tm, ROW_CHUNK)
        for r in range(tm // rc):
            rows = slice(r * rc, (r + 1) * rc)
            h_scr[rows, :] = _rms_norm(x_ref[rows, :], gpre_ref[...]).astype(BF16)
        o_ref[...] = contribution(slice(None))

    @pl.when(jnp.logical_and(f > 0, f < nf - 1))
    def _():
        o_ref[...] += contribution(slice(None))

    @pl.when(f == nf - 1)
    def _():
        rb = min(tm, FFN_LAST_ROW_BLOCK)
        for r in range(tm // rb):
            rows = slice(r * rb, (r + 1) * rb)
            acc = o_ref[rows, :] + contribution(rows)
            o_ref[rows, :] = x_ref[rows, :] + _rms_norm(acc, gpost_ref[...])


def _ffn(x, gpre, wg, wu, wd, gpost, *, layer, tm, tf, cast):
    m, d = x.shape
    dff = wg.shape[-1]
    assert dff // tf >= 2
    col_spec = pl.BlockSpec((d, tf), lambda i, f: (0, f))
    row_spec = pl.BlockSpec((tf, d), lambda i, f: (f, 0))
    out_specs = [pl.BlockSpec((tm, d), lambda i, f: (i, 0))]
    out_shape = [jax.ShapeDtypeStruct((m, d), F32)]
    if cast:
        assert m == tm
        w_specs = [pl.BlockSpec((None, d, tf), lambda i, f: (layer, 0, f)),
                   pl.BlockSpec((None, d, tf), lambda i, f: (layer, 0, f)),
                   pl.BlockSpec((None, tf, d), lambda i, f: (layer, f, 0))]
        out_specs += [col_spec, col_spec, row_spec]
        out_shape += [jax.ShapeDtypeStruct((d, dff), BF16), jax.ShapeDtypeStruct((d, dff), BF16),
                      jax.ShapeDtypeStruct((dff, d), BF16)]
    else:
        w_specs = [col_spec, col_spec, row_spec]
    res = pl.pallas_call(
        functools.partial(_ffn_kernel, cast=cast),
        grid=(m // tm, dff // tf),
        in_specs=[pl.BlockSpec((tm, d), lambda i, f: (i, 0)),
                  _layer_vec_spec(layer, d, 2),
                  *w_specs,
                  _layer_vec_spec(layer, d, 2)],
        out_specs=out_specs,
        out_shape=out_shape,
        scratch_shapes=[pltpu.VMEM((tm, d), BF16)],
        compiler_params=pltpu.CompilerParams(
            dimension_semantics=("arbitrary", "arbitrary"), vmem_limit_bytes=VMEM_LIMIT),
        name="ffn_decode" if cast else "ffn",
    )(x, gpre, wg, wu, wd, gpost)
    return res if cast else res[0]


PROMPT_TM = 1024
OUTPROJ_TM = 512
FFN_TF = 512
DECODE_TB = 16


def kernel(x_prompt, x_sample, state_hgrn, norm_mix_pre, norm_mix_post, w_in, ln_v_gain, ln_v_bias,
           spatial_w, spatial_b, lb_param, hgrn_out_gain, w_out, norm_ffn_pre, norm_ffn_post,
           w_gate, w_up, w_down):
    depth = w_in.shape[0]
    batch, seq_len, d = x_prompt.shape
    dec_batch, dec_seq, _ = x_sample.shape
    assert dec_seq == 1
    assert dec_batch % DECODE_TB == 0 and w_in.shape[2] == 6 * SEG and w_out.shape[1] == 2 * SEG

    lev = jnp.asarray(_level_table())
    vec = lambda a: a.reshape(depth, 1, -1)
    g_mix_pre, g_mix_post = vec(norm_mix_pre), vec(norm_mix_post)
    g_ffn_pre, g_ffn_post = vec(norm_ffn_pre), vec(norm_ffn_post)
    ln_g, ln_b, g_out = vec(ln_v_gain), vec(ln_v_bias), vec(hgrn_out_gain)
    spatial_bt = spatial_b.transpose(0, 2, 1)
    w00 = vec(jnp.repeat(spatial_w[:, :, 0, 0], HEAD_DIM, axis=1))
    b0 = vec(jnp.repeat(spatial_b[:, :, 0], HEAD_DIM, axis=1))

    xp = x_prompt.reshape(batch * seq_len, d)
    xs = x_sample.reshape(dec_batch, d)
    s0t_zero = jnp.zeros((batch, HEADS, HEAD_DIM, HEAD_DIM), F32)
    sp_out, vp_out, vs_out = [], [], []
    ss_all = None

    for l in range(depth):
        oa, vrows, hs, w_in_a = _inproj_a(
            xs, g_mix_pre, w_in, ln_g, ln_b, w00, b0, layer=l, decode=True, tm=dec_batch, seq_len=1)
        qs, lf, kk, ib, gs, w_in_b = _inproj_b(hs, w_in, lb_param, layer=l, decode=True, tm=dec_batch)
        cols = lambda a: a.reshape(dec_batch // DECODE_TB, DECODE_TB, SEG).transpose(0, 2, 1)
        ob, ss_all = _hgrn_decode(cols(lf), cols(kk), cols(qs), ib, gs, g_out, state_hgrn, ss_all,
                                  layer=l, tb=DECODE_TB)
        xs, w_out_b = _outproj_cast(oa, ob, w_out, xs, g_mix_post, layer=l)
        xs, w_gate_b, w_up_b, w_down_b = _ffn(xs, g_ffn_pre, w_gate, w_up, w_down, g_ffn_post,
                                              layer=l, tm=dec_batch, tf=FFN_TF, cast=True)
        vs_out.append(vrows.reshape(dec_batch, 1, SEG))
        oa, vrows, hp = _inproj_a(
            xp, g_mix_pre, w_in_a, ln_g, ln_b, spatial_w, spatial_bt,
            layer=l, decode=False, tm=PROMPT_TM, seq_len=seq_len)
        qs, lf, kk, ib, gs = _inproj_b(hp, w_in_b, lb_param, layer=l, decode=False, tm=PROMPT_TM)
        ob, s_fin = _hgrn_prompt(qs, lf, kk, ib, gs, g_out, s0t_zero, lev,
                                 layer=l, batch=batch, seq_len=seq_len)
        xp = _outproj(oa, ob, w_out_b, xp, g_mix_post, layer=l, tm=OUTPROJ_TM)
        xp = _ffn(xp, g_ffn_pre, w_gate_b, w_up_b, w_down_b, g_ffn_post,
                  layer=l, tm=PROMPT_TM, tf=FFN_TF, cast=False)
        sp_out.append(s_fin)
        vp_out.append(vrows.reshape(batch, CHUNK, SEG))

    return (xp.reshape(batch, seq_len, d), xs.reshape(dec_batch, 1, d),
            jnp.stack(sp_out), ss_all, jnp.stack(vp_out), jnp.stack(vs_out))
```

```python
import functools

import numpy as np
import jax
import jax.numpy as jnp
from jax import lax
from jax.experimental import pallas as pl
from jax.experimental.pallas import tpu as pltpu

F32 = jnp.float32
BF16 = jnp.bfloat16

EPS = 1e-6
LANES = 128
SUBLANES = 8
SEG = 1024
HEADS = 8
HEAD_DIM = SEG // HEADS
HEAD_PAIR = 2 * HEAD_DIM
CHUNK = 128
N_LEVELS = 7
HGRN_STEP_CHUNKS = 4
ROW_CHUNK = 256
COL_CHUNK = 256
V_ROW_BLOCK = 512
FFN_LAST_ROW_BLOCK = 512
LOG2E = np.float32(1.4426950408889634)
VMEM_LIMIT = 60 * 1024 * 1024


def _rms_norm(x, g):
    return x * lax.rsqrt(jnp.mean(x * x, axis=-1, keepdims=True) + EPS) * g


def _sigmoid(x):
    return 1.0 / (1.0 + jnp.exp(-x))


def _gelu(x):
    return 0.5 * x * (1.0 + lax.erf(x * np.float32(np.sqrt(0.5))))


def _dot(a, b):
    return jnp.dot(a, b, preferred_element_type=F32)


def _dot_nt(a, b):
    return lax.dot_general(a, b, (((1,), (1,)), ((), ())), preferred_element_type=F32)


def _dot_tn(a, b):
    return lax.dot_general(a, b, (((0,), (0,)), ((), ())), preferred_element_type=F32)


def _forget_lower_bound(lbp, layer):
    m = jnp.max(lbp, axis=0, keepdims=True)
    e = jnp.exp(lbp - m)
    sm = e / jnp.sum(e, axis=0, keepdims=True)
    cum = sm[0:1]
    for r in range(1, layer + 1):
        cum = cum + sm[r:r + 1]
    return cum - sm[0:1]


def _layer_vec_spec(layer, width, grid_rank):
    if grid_rank == 1:
        return pl.BlockSpec((None, 1, width), lambda i: (layer, 0, 0))
    return pl.BlockSpec((None, 1, width), lambda i, j: (layer, 0, 0))


def _by_columns(h_ref, w_ref, epilogue):
    for c in range(SEG // COL_CHUNK):
        cols = slice(c * COL_CHUNK, (c + 1) * COL_CHUNK)
        epilogue(_dot(h_ref[...], w_ref[:, cols]), cols)


def _inproj_a_kernel(x_ref, gpre_ref, w_ref, lng_ref, lnb_ref, ws_ref, bs_ref,
                     oa_ref, vrows_ref, h_ref, *rest, decode):
    wb_ref = rest[0] if decode else None
    u_scr = rest[-1]
    j = pl.program_id(1)
    tm = x_ref.shape[0]
    if decode:
        wb_ref[...] = w_ref[...].astype(BF16)
    wmat_ref = wb_ref if decode else w_ref

    @pl.when(j == 0)
    def _():
        rc = min(tm, ROW_CHUNK)
        for r in range(tm // rc):
            rows = slice(r * rc, (r + 1) * rc)
            h_ref[rows, :] = _rms_norm(x_ref[rows, :], gpre_ref[...]).astype(BF16)

        def epilogue(acc, cols):
            u_scr[:, cols] = _gelu(acc)
        _by_columns(h_ref, wmat_ref, epilogue)

    @pl.when(j == 1)
    def _():
        if not decode:
            row = lax.broadcasted_iota(jnp.int32, (CHUNK, CHUNK), 0)
            col = lax.broadcasted_iota(jnp.int32, (CHUNK, CHUNK), 1)
            causal = row >= col
        rb = min(tm, V_ROW_BLOCK)
        for r in range(tm // rb):
            rows = slice(r * rb, (r + 1) * rb)
            gv = _gelu(_dot(h_ref[rows, :], wmat_ref[...]))
            mu = jnp.mean(gv, axis=-1, keepdims=True)
            vc = gv - mu
            v = vc * lax.rsqrt(jnp.mean(vc * vc, axis=-1, keepdims=True) + EPS)
            v = v * lng_ref[...] + lnb_ref[...]
            if decode:
                vrows_ref[rows, :] = v
                oa_ref[rows, :] = (u_scr[rows, :] * (v * ws_ref[...] + bs_ref[...])).astype(BF16)
                continue
            if r == tm // rb - 1:
                vrows_ref[...] = v[rb - CHUNK:, :]
            vb = v.astype(BF16)
            for hh in range(HEADS):
                cs = slice(hh * HEAD_DIM, (hh + 1) * HEAD_DIM)
                w_h = jnp.where(causal, ws_ref[hh], 0.0).astype(BF16)
                b_h = bs_ref[:, hh:hh + 1]
                for c in range(rb // CHUNK):
                    s = _dot(w_h, vb[c * CHUNK:(c + 1) * CHUNK, cs]) + b_h
                    rs = slice(rows.start + c * CHUNK, rows.start + (c + 1) * CHUNK)
                    oa_ref[rs, cs] = (u_scr[rs, cs] * s).astype(BF16)


def _inproj_a(x, gpre, w, lng, lnb, ws, bs, *, layer, decode, tm, seq_len):
    m, d = x.shape
    if decode:
        assert m == tm
        vrows_shape, vrows_spec = (m, SEG), pl.BlockSpec((tm, SEG), lambda i, j: (i, 0))
        ws_spec = _layer_vec_spec(layer, SEG, 2)
        bs_spec = _layer_vec_spec(layer, SEG, 2)
        w_spec = pl.BlockSpec((None, d, SEG), lambda i, j: (layer, 0, j))
    else:
        assert seq_len % tm == 0 and tm % CHUNK == 0
        tiles_per_seq = seq_len // tm
        vrows_shape = (m // seq_len * CHUNK, SEG)
        vrows_spec = pl.BlockSpec((CHUNK, SEG), lambda i, j: (i // tiles_per_seq, 0))
        ws_spec = pl.BlockSpec((None, HEADS, CHUNK, CHUNK), lambda i, j: (layer, 0, 0, 0))
        bs_spec = pl.BlockSpec((None, CHUNK, HEADS), lambda i, j: (layer, 0, 0))
        w_spec = pl.BlockSpec((d, SEG), lambda i, j: (0, j))
    out_shape = [jax.ShapeDtypeStruct((m, SEG), BF16),
                 jax.ShapeDtypeStruct(vrows_shape, F32),
                 jax.ShapeDtypeStruct((m, d), BF16)]
    out_specs = [pl.BlockSpec((tm, SEG), lambda i, j: (i, 0)), vrows_spec,
                 pl.BlockSpec((tm, d), lambda i, j: (i, 0))]
    if decode:
        out_shape.append(jax.ShapeDtypeStruct((d, 2 * SEG), BF16))
        out_specs.append(pl.BlockSpec((d, SEG), lambda i, j: (0, j)))
    return pl.pallas_call(
        functools.partial(_inproj_a_kernel, decode=decode),
        grid=(m // tm, 2),
        in_specs=[pl.BlockSpec((tm, d), lambda i, j: (i, 0)),
                  _layer_vec_spec(layer, d, 2),
                  w_spec,
                  _layer_vec_spec(layer, SEG, 2), _layer_vec_spec(layer, SEG, 2), ws_spec, bs_spec],
        out_specs=out_specs,
        out_shape=out_shape,
        scratch_shapes=[pltpu.VMEM((tm, SEG), F32)],
        compiler_params=pltpu.CompilerParams(
            dimension_semantics=("arbitrary", "arbitrary"), vmem_limit_bytes=VMEM_LIMIT),
        name="inproj_a_decode" if decode else "inproj_a",
    )(x, gpre, w, lng, lnb, ws, bs)


def _inproj_b_kernel(h_ref, w_ref, lbp_ref, qs_ref, lf_ref, kk_ref, ib_ref, gs_ref, *rest,
                     layer, decode):
    j = pl.program_id(1)
    if decode:
        wb_ref = rest[0]
        wb_ref[...] = w_ref[...].astype(BF16)
        w_ref = wb_ref

    @pl.when(j == 0)
    def _():
        def epilogue(acc, cols):
            qs_ref[:, cols] = (acc * _sigmoid(acc)).astype(BF16)
        _by_columns(h_ref, w_ref, epilogue)

    @pl.when(j == 1)
    def _():
        lb_all = _forget_lower_bound(lbp_ref[...], layer)

        def epilogue(z, cols):
            ez = jnp.exp(-jnp.abs(z))
            one_plus = 1.0 + ez
            inv = 1.0 / one_plus
            logsig = jnp.minimum(z, 0.0) - jnp.log(one_plus)
            sig_neg = jnp.where(z >= 0.0, ez, 1.0) * inv
            if layer == 0:
                lf_ref[:, cols] = logsig
                kk_ref[:, cols] = sig_neg.astype(BF16)
            else:
                lb = lb_all[:, cols]
                f = lb + (1.0 - lb) * (jnp.where(z >= 0.0, 1.0, ez) * inv)
                lf_ref[:, cols] = jnp.maximum(jnp.log(f), logsig)
                kk_ref[:, cols] = ((1.0 - lb) * sig_neg).astype(BF16)
        _by_columns(h_ref, w_ref, epilogue)

    @pl.when(j == 2)
    def _():
        def epilogue(acc, cols):
            ib_ref[:, cols] = acc.astype(BF16)
        _by_columns(h_ref, w_ref, epilogue)

    @pl.when(j == 3)
    def _():
        def epilogue(acc, cols):
            gs_ref[:, cols] = (acc * _sigmoid(acc)).astype(BF16)
        _by_columns(h_ref, w_ref, epilogue)


def _inproj_b(h, w, lbp, *, layer, decode, tm):
    m, d = h.shape
    n_seg = 4
    if decode:
        assert m == tm
        w_spec = pl.BlockSpec((None, d, SEG), lambda i, j: (layer, 0, j + 2))
    else:
        w_spec = pl.BlockSpec((d, SEG), lambda i, j: (0, j))
    row_spec = pl.BlockSpec((tm, SEG), lambda i, j: (i, 0))
    out_shape = [jax.ShapeDtypeStruct((m, SEG), BF16),
                 jax.ShapeDtypeStruct((m, SEG), F32),
                 jax.ShapeDtypeStruct((m, SEG), BF16),
                 jax.ShapeDtypeStruct((m, SEG), BF16),
                 jax.ShapeDtypeStruct((m, SEG), BF16)]
    out_specs = [row_spec] * 5
    if decode:
        out_shape.append(jax.ShapeDtypeStruct((d, n_seg * SEG), BF16))
        out_specs.append(pl.BlockSpec((d, SEG), lambda i, j: (0, j)))
    return pl.pallas_call(
        functools.partial(_inproj_b_kernel, layer=layer, decode=decode),
        grid=(m // tm, n_seg),
        in_specs=[pl.BlockSpec((tm, d), lambda i, j: (i, 0)),
                  w_spec,
                  pl.BlockSpec(lbp.shape, lambda i, j: (0, 0))],
        out_specs=out_specs,
        out_shape=out_shape,
        compiler_params=pltpu.CompilerParams(
            dimension_semantics=("arbitrary", "arbitrary"), vmem_limit_bytes=VMEM_LIMIT),
        name="inproj_b_decode" if decode else "inproj_b",
    )(h, w, lbp)


def _level_table():
    t = np.arange(CHUNK)[:, None]
    s = np.arange(CHUNK)[None, :]
    x = t ^ s
    lev = np.floor(np.log2(np.maximum(x, 1))).astype(np.int32)
    lev = np.where(t == s, -1, lev)
    lev = np.where(t < s, -2, lev)
    return lev.astype(np.int32)


def _log_gap_to_block_boundary(b, level):
    half = 1 << level
    blk = 2 * half
    k = b.shape[1]
    if half >= SUBLANES:
        bb = b.reshape(CHUNK // blk, blk, k)
        r = bb[:, half - 1:half, :]
        gap = jnp.concatenate([r - bb[:, :half, :], bb[:, half:, :] - r], axis=1)
        return gap.reshape(CHUNK, k)
    b3 = b.reshape(CHUNK // SUBLANES, SUBLANES, k)
    sub = lax.broadcasted_iota(jnp.int32, (1, SUBLANES, k), 1)
    r = None
    for start in range(0, SUBLANES, blk):
        cand = jnp.broadcast_to(b3[:, start + half - 1:start + half, :], b3.shape)
        r = cand if r is None else jnp.where(sub >= start, cand, r)
    sign = jnp.where((sub & half) != 0, 1.0, -1.0)
    return ((b3 - r) * sign).reshape(CHUNK, k)


def _second_half_rows_from(a, b, level):
    half = 1 << level
    blk = 2 * half
    k = a.shape[1]
    if half >= SUBLANES:
        a3 = a.reshape(CHUNK // blk, blk, k)
        b3 = b.reshape(CHUNK // blk, blk, k)
        return jnp.concatenate([b3[:, :half, :], a3[:, half:, :]], axis=1).reshape(CHUNK, k)
    sub = lax.broadcasted_iota(jnp.int32, (1, SUBLANES, k), 1)
    a3 = a.reshape(CHUNK // SUBLANES, SUBLANES, k)
    b3 = b.reshape(CHUNK // SUBLANES, SUBLANES, k)
    return jnp.where((sub & half) != 0, a3, b3).reshape(CHUNK, k)


def _block_diag(x0, x1):
    z = jnp.zeros_like(x0)
    return jnp.concatenate([jnp.concatenate([x0, z], axis=1), jnp.concatenate([z, x1], axis=1)], axis=0)


def _halves(x):
    return x[:, :HEAD_DIM], x[:, HEAD_DIM:]


def _hgrn_chunk(qs_ref, lf_ref, kk_ref, ib_ref, gs_ref, gout_ref, ob_ref, st_scr, lev2, tri):
    odd_row = (lax.broadcasted_iota(jnp.int32, (CHUNK, HEAD_PAIR), 0) & 1) == 1
    pairs = [slice(p * HEAD_PAIR, (p + 1) * HEAD_PAIR) for p in range(HEADS // 2)]
    lf2, q, kk, b, o, a = [], [], [], [], [], []
    for p, cp in enumerate(pairs):
        lf2.append(lf_ref[:, cp] * LOG2E)
        hi = lf2[p].astype(BF16)
        r1 = lf2[p] - hi.astype(F32)
        mid = r1.astype(BF16)
        lo = (r1 - mid.astype(F32)).astype(BF16)
        b.append(_dot(tri, hi) + _dot(tri, mid) + _dot(tri, lo))
    for p, cp in enumerate(pairs):
        qb = qs_ref[:, cp]
        kb = kk_ref[:, cp]
        q.append(qb.astype(F32))
        kk.append(kb.astype(F32))
        o.append(_dot_nt((q[p] * jnp.exp2(b[p])).astype(BF16), st_scr[p].astype(BF16)))
        a.append(jnp.where(lev2 == -1, _dot_nt(qb, _block_diag(*_halves(kb))), 0.0))

    for level in range(N_LEVELS):
        for p in range(HEADS // 2):
            if level == 0:
                d = jnp.where(odd_row, lf2[p], 0.0)
            else:
                d = _log_gap_to_block_boundary(b[p], level)
            x = (_second_half_rows_from(q[p], kk[p], level) * jnp.exp2(d)).astype(BF16)
            a[p] = jnp.where(lev2 == level, _dot_nt(x, _block_diag(*_halves(x))), a[p])

    for p, cp in enumerate(pairs):
        o[p] = o[p] + _dot(a[p].astype(BF16), _block_diag(*_halves(ib_ref[:, cp])))
    for p, cp in enumerate(pairs):
        b_last = b[p][CHUNK - 1:CHUNK, :]
        kd = (kk[p] * jnp.exp2(b_last - b[p])).astype(BF16)
        upd = _dot_tn(ib_ref[:, cp], kd)
        decay = jnp.exp2(b_last)
        for blk in (slice(0, HEAD_DIM), slice(HEAD_DIM, HEAD_PAIR)):
            st_scr[p, blk, blk] = st_scr[p, blk, blk] * decay[:, blk] + upd[blk, blk]
    for p, cp in enumerate(pairs):
        for blk in (slice(0, HEAD_DIM), slice(HEAD_DIM, HEAD_PAIR)):
            oh = o[p][:, blk]
            cs = slice(cp.start + blk.start, cp.start + blk.stop)
            on = oh * lax.rsqrt(jnp.mean(oh * oh, axis=-1, keepdims=True) + EPS)
            ob_ref[:, cs] = (on * gout_ref[:, cs] * gs_ref[:, cs].astype(F32)).astype(BF16)


def _hgrn_prompt_kernel(qs_ref, lf_ref, kk_ref, ib_ref, gs_ref, gout_ref, s0t_ref, lev_ref,
                        ob_ref, sfin_ref, st_scr):
    c = pl.program_id(1)

    @pl.when(c == 0)
    def _():
        for p in range(HEADS // 2):
            st_scr[p] = _block_diag(s0t_ref[0, 2 * p], s0t_ref[0, 2 * p + 1])

    lev = lev_ref[...]
    lev2 = jnp.concatenate([lev, lev], axis=1)
    row = lax.broadcasted_iota(jnp.int32, (CHUNK, CHUNK), 0)
    col = lax.broadcasted_iota(jnp.int32, (CHUNK, CHUNK), 1)
    tri = jnp.where(row >= col, 1.0, 0.0).astype(BF16)

    for ci in range(lf_ref.shape[0] // CHUNK):
        rows = pl.ds(ci * CHUNK, CHUNK)
        _hgrn_chunk(qs_ref.at[rows], lf_ref.at[rows], kk_ref.at[rows], ib_ref.at[rows],
                    gs_ref.at[rows], gout_ref, ob_ref.at[rows], st_scr, lev2, tri)

    @pl.when(c == pl.num_programs(1) - 1)
    def _():
        for p in range(HEADS // 2):
            sfin_ref[0, 2 * p] = st_scr[p, :HEAD_DIM, :HEAD_DIM].T
            sfin_ref[0, 2 * p + 1] = st_scr[p, HEAD_DIM:, HEAD_DIM:].T


def _hgrn_prompt(qs, lf, kk, ib, gs, gout, s0t, lev, *, layer, batch, seq_len):
    m = qs.shape[0]
    rows = HGRN_STEP_CHUNKS * CHUNK
    assert seq_len % rows == 0
    nc = seq_len // rows
    row_spec = pl.BlockSpec((rows, SEG), lambda b, c: (b * nc + c, 0))
    state_spec = pl.BlockSpec((1, HEADS, HEAD_DIM, HEAD_DIM), lambda b, c: (b, 0, 0, 0))
    return pl.pallas_call(
        _hgrn_prompt_kernel,
        grid=(batch, nc),
        in_specs=[row_spec, row_spec, row_spec, row_spec, row_spec,
                  _layer_vec_spec(layer, SEG, 2),
                  state_spec,
                  pl.BlockSpec((CHUNK, CHUNK), lambda b, c: (0, 0))],
        out_specs=[row_spec, state_spec],
        out_shape=[jax.ShapeDtypeStruct((m, SEG), BF16),
                   jax.ShapeDtypeStruct((batch, HEADS, HEAD_DIM, HEAD_DIM), F32)],
        scratch_shapes=[pltpu.VMEM((HEADS // 2, HEAD_PAIR, HEAD_PAIR), F32)],
        compiler_params=pltpu.CompilerParams(
            dimension_semantics=("arbitrary", "arbitrary"), vmem_limit_bytes=VMEM_LIMIT),
        name="hgrn_prompt",
    )(qs, lf, kk, ib, gs, gout, s0t, lev)


def _hgrn_decode_kernel(lft_ref, kkt_ref, qst_ref, ib_ref, gs_ref, gout_ref, s0_ref, *rest):
    ob_ref, snew_ref, o_scr = rest[-3:]
    tb = s0_ref.shape[0]
    for bb in range(tb):
        for hh in range(HEADS):
            ks = slice(hh * HEAD_DIM, (hh + 1) * HEAD_DIM)
            f_col = jnp.exp(lft_ref[0, ks, bb:bb + 1])
            k_col = kkt_ref[0, ks, bb:bb + 1].astype(F32)
            q_col = qst_ref[0, ks, bb:bb + 1].astype(F32)
            i_row = ib_ref[bb:bb + 1, ks].astype(F32)
            s_new = f_col * s0_ref[bb, hh] + k_col * i_row
            snew_ref[bb, hh] = s_new
            o_scr[bb:bb + 1, ks] = jnp.sum(q_col * s_new, axis=0, keepdims=True)
    for hh in range(HEADS):
        ks = slice(hh * HEAD_DIM, (hh + 1) * HEAD_DIM)
        o = o_scr[:, ks]
        on = o * lax.rsqrt(jnp.mean(o * o, axis=-1, keepdims=True) + EPS)
        ob_ref[:, ks] = (on * gout_ref[:, ks] * gs_ref[:, ks].astype(F32)).astype(BF16)


def _hgrn_decode(lft, kkt, qst, ib, gs, gout, s0_all, s_new_all, *, layer, tb):
    m = ib.shape[0]
    col_spec = pl.BlockSpec((1, SEG, tb), lambda i: (i, 0, 0))
    row_spec = pl.BlockSpec((tb, SEG), lambda i: (i, 0))
    state_spec = pl.BlockSpec((None, tb, HEADS, HEAD_DIM, HEAD_DIM), lambda i: (layer, i, 0, 0, 0))
    in_specs = [col_spec, col_spec, col_spec, row_spec, row_spec,
                _layer_vec_spec(layer, SEG, 1), state_spec]
    args = [lft, kkt, qst, ib, gs, gout, s0_all]
    aliases = {}
    if s_new_all is not None:
        in_specs.append(pl.BlockSpec(memory_space=pl.ANY))
        args.append(s_new_all)
        aliases = {len(args) - 1: 1}
    return pl.pallas_call(
        _hgrn_decode_kernel,
        grid=(m // tb,),
        in_specs=in_specs,
        out_specs=[row_spec, state_spec],
        out_shape=[jax.ShapeDtypeStruct((m, SEG), BF16),
                   jax.ShapeDtypeStruct(s0_all.shape, F32)],
        input_output_aliases=aliases,
        scratch_shapes=[pltpu.VMEM((tb, SEG), F32)],
        compiler_params=pltpu.CompilerParams(
            dimension_semantics=("arbitrary",), vmem_limit_bytes=VMEM_LIMIT),
        name="hgrn_decode",
    )(*args)


def _outproj_kernel(oa_ref, ob_ref, w_ref, x_ref, g_ref, o_ref):
    mix = _dot(oa_ref[...], w_ref[:SEG, :]) + _dot(ob_ref[...], w_ref[SEG:, :])
    o_ref[...] = x_ref[...] + _rms_norm(mix, g_ref[...])


def _outproj_cast_kernel(oa_ref, ob_ref, w_ref, x_ref, g_ref, o_ref, wb_ref, acc_scr):
    k = pl.program_id(0)
    wb = w_ref[...].astype(BF16)
    wb_ref[...] = wb

    @pl.when(k == 0)
    def _():
        acc_scr[...] = _dot(oa_ref[...], wb)

    @pl.when(k == 1)
    def _():
        o_ref[...] = x_ref[...] + _rms_norm(acc_scr[...] + _dot(ob_ref[...], wb), g_ref[...])


def _outproj_cast(oa, ob, w, x, g, *, layer):
    m, d = x.shape
    whole = lambda shape: pl.BlockSpec(shape, lambda k: (0,) * len(shape))
    return pl.pallas_call(
        _outproj_cast_kernel,
        grid=(2,),
        in_specs=[whole((m, SEG)), whole((m, SEG)),
                  pl.BlockSpec((None, SEG, d), lambda k: (layer, k, 0)),
                  whole((m, d)),
                  _layer_vec_spec(layer, d, 1)],
        out_specs=[whole((m, d)), pl.BlockSpec((SEG, d), lambda k: (k, 0))],
        out_shape=[jax.ShapeDtypeStruct((m, d), F32),
                   jax.ShapeDtypeStruct((2 * SEG, d), BF16)],
        scratch_shapes=[pltpu.VMEM((m, d), F32)],
        compiler_params=pltpu.CompilerParams(
            dimension_semantics=("arbitrary",), vmem_limit_bytes=VMEM_LIMIT),
        name="outproj_decode",
    )(oa, ob, w, x, g)


def _outproj(oa, ob, w, x, g, *, layer, tm):
    m, d = x.shape
    return pl.pallas_call(
        _outproj_kernel,
        grid=(m // tm,),
        in_specs=[pl.BlockSpec((tm, SEG), lambda i: (i, 0)),
                  pl.BlockSpec((tm, SEG), lambda i: (i, 0)),
                  pl.BlockSpec(w.shape, lambda i: (0, 0)),
                  pl.BlockSpec((tm, d), lambda i: (i, 0)),
                  _layer_vec_spec(layer, d, 1)],
        out_specs=pl.BlockSpec((tm, d), lambda i: (i, 0)),
        out_shape=jax.ShapeDtypeStruct((m, d), F32),
        compiler_params=pltpu.CompilerParams(
            dimension_semantics=("arbitrary",), vmem_limit_bytes=VMEM_LIMIT),
        name="outproj",
    )(oa, ob, w, x, g)


def _ffn_kernel(x_ref, gpre_ref, wg_ref, wu_ref, wd_ref, gpost_ref, o_ref, *rest, cast):
    h_scr = rest[-1]
    f = pl.program_id(1)
    nf = pl.num_programs(1)
    tm = x_ref.shape[0]
    if cast:
        for src, dst in zip((wg_ref, wu_ref, wd_ref), rest[:3]):
            dst[...] = src[...].astype(BF16)
        wg_ref, wu_ref, wd_ref = rest[:3]

    def contribution(rows):
        h = h_scr[rows, :]
        gate = _dot(h, wg_ref[...])
        up = _dot(h, wu_ref[...])
        t = (gate * _sigmoid(gate) * up).astype(BF16)
        return _dot(t, wd_ref[...])

    @pl.when(f == 0)
    def _():
        rc = min(tm, ROW_CHUNK)
        for r in range(tm // rc):
            rows = slice(r * rc, (r + 1) * rc)
            h_scr[rows, :] = _rms_norm(x_ref[rows, :], gpre_ref[...]).astype(BF16)
        o_ref[...] = contribution(slice(None))

    @pl.when(jnp.logical_and(f > 0, f < nf - 1))
    def _():
        o_ref[...] += contribution(slice(None))

    @pl.when(f == nf - 1)
    def _():
        rb = min(tm, FFN_LAST_ROW_BLOCK)
        for r in range(tm // rb):
            rows = slice(r * rb, (r + 1) * rb)
            acc = o_ref[rows, :] + contribution(rows)
            o_ref[rows, :] = x_ref[rows, :] + _rms_norm(acc, gpost_ref[...])


def _ffn(x, gpre, wg, wu, wd, gpost, *, layer, tm, tf, cast):
    m, d = x.shape
    dff = wg.shape[-1]
    assert dff // tf >= 2
    col_spec = pl.BlockSpec((d, tf), lambda i, f: (0, f))
    row_spec = pl.BlockSpec((tf, d), lambda i, f: (f, 0))
    out_specs = [pl.BlockSpec((tm, d), lambda i, f: (i, 0))]
    out_shape = [jax.ShapeDtypeStruct((m, d), F32)]
    if cast:
        assert m == tm
        w_specs = [pl.BlockSpec((None, d, tf), lambda i, f: (layer, 0, f)),
                   pl.BlockSpec((None, d, tf), lambda i, f: (layer, 0, f)),
                   pl.BlockSpec((None, tf, d), lambda i, f: (layer, f, 0))]
        out_specs += [col_spec, col_spec, row_spec]
        out_shape += [jax.ShapeDtypeStruct((d, dff), BF16), jax.ShapeDtypeStruct((d, dff), BF16),
                      jax.ShapeDtypeStruct((dff, d), BF16)]
    else:
        w_specs = [col_spec, col_spec, row_spec]
    res = pl.pallas_call(
        functools.partial(_ffn_kernel, cast=cast),
        grid=(m // tm, dff // tf),
        in_specs=[pl.BlockSpec((tm, d), lambda i, f: (i, 0)),
                  _layer_vec_spec(layer, d, 2),
                  *w_specs,
                  _layer_vec_spec(layer, d, 2)],
        out_specs=out_specs,
        out_shape=out_shape,
        scratch_shapes=[pltpu.VMEM((tm, d), BF16)],
        compiler_params=pltpu.CompilerParams(
            dimension_semantics=("arbitrary", "arbitrary"), vmem_limit_bytes=VMEM_LIMIT),
        name="ffn_decode" if cast else "ffn",
    )(x, gpre, wg, wu, wd, gpost)
    return res if cast else res[0]


PROMPT_TM = 1024
OUTPROJ_TM = 512
FFN_TF = 512
DECODE_TB = 16


def kernel(x_prompt, x_sample, state_hgrn, norm_mix_pre, norm_mix_post, w_in, ln_v_gain, ln_v_bias,
           spatial_w, spatial_b, lb_param, hgrn_out_gain, w_out, norm_ffn_pre, norm_ffn_post,
           w_gate, w_up, w_down):
    depth = w_in.shape[0]
    batch, seq_len, d = x_prompt.shape
    dec_batch, dec_seq, _ = x_sample.shape
    assert dec_seq == 1
    assert dec_batch % DECODE_TB == 0 and w_in.shape[2] == 6 * SEG and w_out.shape[1] == 2 * SEG

    lev = jnp.asarray(_level_table())
    vec = lambda a: a.reshape(depth, 1, -1)
    g_mix_pre, g_mix_post = vec(norm_mix_pre), vec(norm_mix_post)
    g_ffn_pre, g_ffn_post = vec(norm_ffn_pre), vec(norm_ffn_post)
    ln_g, ln_b, g_out = vec(ln_v_gain), vec(ln_v_bias), vec(hgrn_out_gain)
    spatial_bt = spatial_b.transpose(0, 2, 1)
    w00 = vec(jnp.repeat(spatial_w[:, :, 0, 0], HEAD_DIM, axis=1))
    b0 = vec(jnp.repeat(spatial_b[:, :, 0], HEAD_DIM, axis=1))

    xp = x_prompt.reshape(batch * seq_len, d)
    xs = x_sample.reshape(dec_batch, d)
    s0t_zero = jnp.zeros((batch, HEADS, HEAD_DIM, HEAD_DIM), F32)
    sp_out, vp_out, vs_out = [], [], []
    ss_all = None

    for l in range(depth):
        oa, vrows, hs, w_in_a = _inproj_a(
            xs, g_mix_pre, w_in, ln_g, ln_b, w00, b0, layer=l, decode=True, tm=dec_batch, seq_len=1)
        qs, lf, kk, ib, gs, w_in_b = _inproj_b(hs, w_in, lb_param, layer=l, decode=True, tm=dec_batch)
        cols = lambda a: a.reshape(dec_batch // DECODE_TB, DECODE_TB, SEG).transpose(0, 2, 1)
        ob, ss_all = _hgrn_decode(cols(lf), cols(kk), cols(qs), ib, gs, g_out, state_hgrn, ss_all,
                                  layer=l, tb=DECODE_TB)
        xs, w_out_b = _outproj_cast(oa, ob, w_out, xs, g_mix_post, layer=l)
        xs, w_gate_b, w_up_b, w_down_b = _ffn(xs, g_ffn_pre, w_gate, w_up, w_down, g_ffn_post,
                                              layer=l, tm=dec_batch, tf=FFN_TF, cast=True)
        vs_out.append(vrows.reshape(dec_batch, 1, SEG))
        oa, vrows, hp = _inproj_a(
            xp, g_mix_pre, w_in_a, ln_g, ln_b, spatial_w, spatial_bt,
            layer=l, decode=False, tm=PROMPT_TM, seq_len=seq_len)
        qs, lf, kk, ib, gs = _inproj_b(hp, w_in_b, lb_param, layer=l, decode=False, tm=PROMPT_TM)
        ob, s_fin = _hgrn_prompt(qs, lf, kk, ib, gs, g_out, s0t_zero, lev,
                                 layer=l, batch=batch, seq_len=seq_len)
        xp = _outproj(oa, ob, w_out_b, xp, g_mix_post, layer=l, tm=OUTPROJ_TM)
        xp = _ffn(xp, g_ffn_pre, w_gate_b, w_up_b, w_down_b, g_ffn_post,
                  layer=l, tm=PROMPT_TM, tf=FFN_TF, cast=False)
        sp_out.append(s_fin)
        vp_out.append(vrows.reshape(batch, CHUNK, SEG))

    return (xp.reshape(batch, seq_len, d), xs.reshape(dec_batch, 1, d),
            jnp.stack(sp_out), ss_all, jnp.stack(vp_out), jnp.stack(vs_out))
```

```python
import functools

import numpy as np
import jax
import jax.numpy as jnp
from jax import lax
from jax.experimental import pallas as pl
from jax.experimental.pallas import tpu as pltpu

F32 = jnp.float32
BF16 = jnp.bfloat16

EPS = 1e-6
LANES = 128
SUBLANES = 8
SEG = 1024
HEADS = 8
HEAD_DIM = SEG // HEADS
CHUNK = 128
N_LEVELS = 7
HGRN_STEP_CHUNKS = 4
ROW_CHUNK = 256
COL_CHUNK = 256
V_ROW_BLOCK = 512
FFN_LAST_ROW_BLOCK = 512
LOG2E = np.float32(1.4426950408889634)
VMEM_LIMIT = 60 * 1024 * 1024


def _rms_norm(x, g):
    return x * lax.rsqrt(jnp.mean(x * x, axis=-1, keepdims=True) + EPS) * g


def _silu(x):
    return x * (0.5 + 0.5 * jnp.tanh(0.5 * x))


def _gelu(x):
    return 0.5 * x * (1.0 + lax.erf(x * np.float32(np.sqrt(0.5))))


def _dot(a, b):
    return jnp.dot(a, b, preferred_element_type=F32)


def _dot_nt(a, b):
    return lax.dot_general(a, b, (((1,), (1,)), ((), ())), preferred_element_type=F32)


def _dot_tn(a, b):
    return lax.dot_general(a, b, (((0,), (0,)), ((), ())), preferred_element_type=F32)


def _forget_lower_bound(lbp, layer):
    m = jnp.max(lbp, axis=0, keepdims=True)
    e = jnp.exp(lbp - m)
    sm = e / jnp.sum(e, axis=0, keepdims=True)
    cum = sm[0:1]
    for r in range(1, layer + 1):
        cum = cum + sm[r:r + 1]
    return cum - sm[0:1]


def _layer_vec_spec(layer, width, grid_rank):
    if grid_rank == 1:
        return pl.BlockSpec((None, 1, width), lambda i: (layer, 0, 0))
    return pl.BlockSpec((None, 1, width), lambda i, j: (layer, 0, 0))


def _by_columns(h_ref, w_ref, epilogue):
    for c in range(SEG // COL_CHUNK):
        cols = slice(c * COL_CHUNK, (c + 1) * COL_CHUNK)
        epilogue(_dot(h_ref[...], w_ref[:, cols]), cols)


def _inproj_a_kernel(x_ref, gpre_ref, w_ref, lng_ref, lnb_ref, ws_ref, bs_ref,
                     oa_ref, vrows_ref, h_ref, *rest, decode):
    wb_ref = rest[0] if decode else None
    u_scr = rest[-1]
    j = pl.program_id(1)
    tm = x_ref.shape[0]
    if decode:
        wb_ref[...] = w_ref[...].astype(BF16)
    wmat_ref = wb_ref if decode else w_ref

    @pl.when(j == 0)
    def _():
        rc = min(tm, ROW_CHUNK)
        for r in range(tm // rc):
            rows = slice(r * rc, (r + 1) * rc)
            h_ref[rows, :] = _rms_norm(x_ref[rows, :], gpre_ref[...]).astype(BF16)

        def epilogue(acc, cols):
            u_scr[:, cols] = _gelu(acc)
        _by_columns(h_ref, wmat_ref, epilogue)

    @pl.when(j == 1)
    def _():
        if not decode:
            row = lax.broadcasted_iota(jnp.int32, (CHUNK, CHUNK), 0)
            col = lax.broadcasted_iota(jnp.int32, (CHUNK, CHUNK), 1)
            causal = row >= col
        rb = min(tm, V_ROW_BLOCK)
        for r in range(tm // rb):
            rows = slice(r * rb, (r + 1) * rb)
            gv = _gelu(_dot(h_ref[rows, :], wmat_ref[...]))
            mu = jnp.mean(gv, axis=-1, keepdims=True)
            vc = gv - mu
            v = vc * lax.rsqrt(jnp.mean(vc * vc, axis=-1, keepdims=True) + EPS)
            v = v * lng_ref[...] + lnb_ref[...]
            if decode:
                vrows_ref[rows, :] = v
                oa_ref[rows, :] = (u_scr[rows, :] * (v * ws_ref[...] + bs_ref[...])).astype(BF16)
                continue
            if r == tm // rb - 1:
                vrows_ref[...] = v[rb - CHUNK:, :]
            vb = v.astype(BF16)
            for hh in range(HEADS):
                cs = slice(hh * HEAD_DIM, (hh + 1) * HEAD_DIM)
                w_h = jnp.where(causal, ws_ref[hh], 0.0).astype(BF16)
                b_h = bs_ref[:, hh:hh + 1]
                for c in range(rb // CHUNK):
                    s = _dot(w_h, vb[c * CHUNK:(c + 1) * CHUNK, cs]) + b_h
                    rs = slice(rows.start + c * CHUNK, rows.start + (c + 1) * CHUNK)
                    oa_ref[rs, cs] = (u_scr[rs, cs] * s).astype(BF16)


def _inproj_a(x, gpre, w, lng, lnb, ws, bs, *, layer, decode, tm, seq_len):
    m, d = x.shape
    if decode:
        assert m == tm
        vrows_shape, vrows_spec = (m, SEG), pl.BlockSpec((tm, SEG), lambda i, j: (i, 0))
        ws_spec = _layer_vec_spec(layer, SEG, 2)
        bs_spec = _layer_vec_spec(layer, SEG, 2)
        w_spec = pl.BlockSpec((None, d, SEG), lambda i, j: (layer, 0, j))
    else:
        assert seq_len % tm == 0 and tm % CHUNK == 0
        tiles_per_seq = seq_len // tm
        vrows_shape = (m // seq_len * CHUNK, SEG)
        vrows_spec = pl.BlockSpec((CHUNK, SEG), lambda i, j: (i // tiles_per_seq, 0))
        ws_spec = pl.BlockSpec((None, HEADS, CHUNK, CHUNK), lambda i, j: (layer, 0, 0, 0))
        bs_spec = pl.BlockSpec((None, CHUNK, HEADS), lambda i, j: (layer, 0, 0))
        w_spec = pl.BlockSpec((d, SEG), lambda i, j: (0, j))
    out_shape = [jax.ShapeDtypeStruct((m, SEG), BF16),
                 jax.ShapeDtypeStruct(vrows_shape, F32),
                 jax.ShapeDtypeStruct((m, d), BF16)]
    out_specs = [pl.BlockSpec((tm, SEG), lambda i, j: (i, 0)), vrows_spec,
                 pl.BlockSpec((tm, d), lambda i, j: (i, 0))]
    if decode:
        out_shape.append(jax.ShapeDtypeStruct((d, 2 * SEG), BF16))
        out_specs.append(pl.BlockSpec((d, SEG), lambda i, j: (0, j)))
    return pl.pallas_call(
        functools.partial(_inproj_a_kernel, decode=decode),
        grid=(m // tm, 2),
        in_specs=[pl.BlockSpec((tm, d), lambda i, j: (i, 0)),
                  _layer_vec_spec(layer, d, 2),
                  w_spec,
                  _layer_vec_spec(layer, SEG, 2), _layer_vec_spec(layer, SEG, 2), ws_spec, bs_spec],
        out_specs=out_specs,
        out_shape=out_shape,
        scratch_shapes=[pltpu.VMEM((tm, SEG), F32)],
        compiler_params=pltpu.CompilerParams(
            dimension_semantics=("arbitrary", "arbitrary"), vmem_limit_bytes=VMEM_LIMIT),
        name="inproj_a_decode" if decode else "inproj_a",
    )(x, gpre, w, lng, lnb, ws, bs)


def _inproj_b_kernel(h_ref, w_ref, lbp_ref, qs_ref, lf_ref, kk_ref, ib_ref, gs_ref, *rest,
                     layer, decode):
    j = pl.program_id(1)
    if decode:
        wb_ref = rest[0]
        wb_ref[...] = w_ref[...].astype(BF16)
        w_ref = wb_ref

    @pl.when(j == 0)
    def _():
        def epilogue(acc, cols):
            qs_ref[:, cols] = _silu(acc).astype(BF16)
        _by_columns(h_ref, w_ref, epilogue)

    @pl.when(j == 1)
    def _():
        lb_all = _forget_lower_bound(lbp_ref[...], layer)

        def epilogue(z, cols):
            ez = jnp.exp(-jnp.abs(z))
            one_plus = 1.0 + ez
            inv = 1.0 / one_plus
            sig_neg = jnp.where(z >= 0.0, ez, 1.0) * inv
            if layer == 0:
                lf_ref[:, cols] = jnp.minimum(z, 0.0) - jnp.log(one_plus)
                kk_ref[:, cols] = sig_neg.astype(BF16)
            else:
                lb = lb_all[:, cols]
                f = lb + (1.0 - lb) * (jnp.where(z >= 0.0, 1.0, ez) * inv)
                lf_ref[:, cols] = jnp.where(f > 0.0, jnp.log(f), jnp.minimum(z, 0.0))
                kk_ref[:, cols] = ((1.0 - lb) * sig_neg).astype(BF16)
        _by_columns(h_ref, w_ref, epilogue)

    @pl.when(j == 2)
    def _():
        def epilogue(acc, cols):
            ib_ref[:, cols] = acc.astype(BF16)
        _by_columns(h_ref, w_ref, epilogue)

    @pl.when(j == 3)
    def _():
        def epilogue(acc, cols):
            gs_ref[:, cols] = _silu(acc).astype(BF16)
        _by_columns(h_ref, w_ref, epilogue)


def _inproj_b(h, w, lbp, *, layer, decode, tm):
    m, d = h.shape
    n_seg = 4
    if decode:
        assert m == tm
        w_spec = pl.BlockSpec((None, d, SEG), lambda i, j: (layer, 0, j + 2))
    else:
        w_spec = pl.BlockSpec((d, SEG), lambda i, j: (0, j))
    row_spec = pl.BlockSpec((tm, SEG), lambda i, j: (i, 0))
    out_shape = [jax.ShapeDtypeStruct((m, SEG), BF16),
                 jax.ShapeDtypeStruct((m, SEG), F32),
                 jax.ShapeDtypeStruct((m, SEG), BF16),
                 jax.ShapeDtypeStruct((m, SEG), BF16),
                 jax.ShapeDtypeStruct((m, SEG), BF16)]
    out_specs = [row_spec] * 5
    if decode:
        out_shape.append(jax.ShapeDtypeStruct((d, n_seg * SEG), BF16))
        out_specs.append(pl.BlockSpec((d, SEG), lambda i, j: (0, j)))
    return pl.pallas_call(
        functools.partial(_inproj_b_kernel, layer=layer, decode=decode),
        grid=(m // tm, n_seg),
        in_specs=[pl.BlockSpec((tm, d), lambda i, j: (i, 0)),
                  w_spec,
                  pl.BlockSpec(lbp.shape, lambda i, j: (0, 0))],
        out_specs=out_specs,
        out_shape=out_shape,
        compiler_params=pltpu.CompilerParams(
            dimension_semantics=("arbitrary", "arbitrary"), vmem_limit_bytes=VMEM_LIMIT),
        name="inproj_b_decode" if decode else "inproj_b",
    )(h, w, lbp)


def _level_table():
    t = np.arange(CHUNK)[:, None]
    s = np.arange(CHUNK)[None, :]
    x = t ^ s
    lev = np.floor(np.log2(np.maximum(x, 1))).astype(np.int32)
    lev = np.where(t == s, -1, lev)
    lev = np.where(t < s, -2, lev)
    return lev.astype(np.int32)


def _log_gap_to_block_boundary(b, level):
    half = 1 << level
    blk = 2 * half
    k = b.shape[1]
    if half >= SUBLANES:
        bb = b.reshape(CHUNK // blk, blk, k)
        r = bb[:, half - 1:half, :]
        gap = jnp.concatenate([r - bb[:, :half, :], bb[:, half:, :] - r], axis=1)
        return gap.reshape(CHUNK, k)
    b3 = b.reshape(CHUNK // SUBLANES, SUBLANES, k)
    sub = lax.broadcasted_iota(jnp.int32, (1, SUBLANES, k), 1)
    r = None
    for start in range(0, SUBLANES, blk):
        cand = jnp.broadcast_to(b3[:, start + half - 1:start + half, :], b3.shape)
        r = cand if r is None else jnp.where(sub >= start, cand, r)
    sign = jnp.where((sub & half) != 0, 1.0, -1.0)
    return ((b3 - r) * sign).reshape(CHUNK, k)


def _second_half_rows_from(a, b, level):
    half = 1 << level
    blk = 2 * half
    k = a.shape[1]
    if half >= SUBLANES:
        a3 = a.reshape(CHUNK // blk, blk, k)
        b3 = b.reshape(CHUNK // blk, blk, k)
        return jnp.concatenate([b3[:, :half, :], a3[:, half:, :]], axis=1).reshape(CHUNK, k)
    sub = lax.broadcasted_iota(jnp.int32, (1, SUBLANES, k), 1)
    a3 = a.reshape(CHUNK // SUBLANES, SUBLANES, k)
    b3 = b.reshape(CHUNK // SUBLANES, SUBLANES, k)
    return jnp.where((sub & half) != 0, a3, b3).reshape(CHUNK, k)


def _hgrn_chunk(qs_ref, lf_ref, kk_ref, ib_ref, gs_ref, gout_ref, ob_ref, st_scr, lev, tri):
    odd_row = (lax.broadcasted_iota(jnp.int32, (CHUNK, HEAD_DIM), 0) & 1) == 1
    heads = [slice(hh * HEAD_DIM, (hh + 1) * HEAD_DIM) for hh in range(HEADS)]
    lf2, q, kk, b, o, a = [], [], [], [], [], []
    for hh, cs in enumerate(heads):
        lf2.append(lf_ref[:, cs] * LOG2E)
        hi = lf2[hh].astype(BF16)
        r1 = lf2[hh] - hi.astype(F32)
        mid = r1.astype(BF16)
        lo = (r1 - mid.astype(F32)).astype(BF16)
        b.append(_dot(tri, hi) + _dot(tri, mid) + _dot(tri, lo))
    for hh, cs in enumerate(heads):
        qb = qs_ref[:, cs]
        kb = kk_ref[:, cs]
        q.append(qb.astype(F32))
        kk.append(kb.astype(F32))
        o.append(_dot_nt((q[hh] * jnp.exp2(b[hh])).astype(BF16), st_scr[hh].astype(BF16)))
        a.append(jnp.where(lev == -1, _dot_nt(qb, kb), 0.0))

    for level in range(N_LEVELS):
        for hh in range(HEADS):
            if level == 0:
                d = jnp.where(odd_row, lf2[hh], 0.0)
            else:
                d = _log_gap_to_block_boundary(b[hh], level)
            x = (_second_half_rows_from(q[hh], kk[hh], level) * jnp.exp2(d)).astype(BF16)
            a[hh] = jnp.where(lev == level, _dot_nt(x, x), a[hh])

    for hh, cs in enumerate(heads):
        o[hh] = o[hh] + _dot(a[hh].astype(BF16), ib_ref[:, cs])
    for hh, cs in enumerate(heads):
        b_last = b[hh][CHUNK - 1:CHUNK, :]
        kd = (kk[hh] * jnp.exp2(b_last - b[hh])).astype(BF16)
        st_scr[hh] = st_scr[hh] * jnp.exp2(b_last) + _dot_tn(ib_ref[:, cs], kd)
    for hh, cs in enumerate(heads):
        on = o[hh] * lax.rsqrt(jnp.mean(o[hh] * o[hh], axis=-1, keepdims=True) + EPS)
        ob_ref[:, cs] = (on * gout_ref[:, cs] * gs_ref[:, cs].astype(F32)).astype(BF16)


def _hgrn_prompt_kernel(qs_ref, lf_ref, kk_ref, ib_ref, gs_ref, gout_ref, s0t_ref, lev_ref,
                        ob_ref, sfin_ref, st_scr):
    c = pl.program_id(1)

    @pl.when(c == 0)
    def _():
        st_scr[...] = s0t_ref[0]

    lev = lev_ref[...]
    row = lax.broadcasted_iota(jnp.int32, (CHUNK, CHUNK), 0)
    col = lax.broadcasted_iota(jnp.int32, (CHUNK, CHUNK), 1)
    tri = jnp.where(row >= col, 1.0, 0.0).astype(BF16)

    for ci in range(lf_ref.shape[0] // CHUNK):
        rows = pl.ds(ci * CHUNK, CHUNK)
        _hgrn_chunk(qs_ref.at[rows], lf_ref.at[rows], kk_ref.at[rows], ib_ref.at[rows],
                    gs_ref.at[rows], gout_ref, ob_ref.at[rows], st_scr, lev, tri)

    @pl.when(c == pl.num_programs(1) - 1)
    def _():
        for hh in range(HEADS):
            sfin_ref[0, hh] = st_scr[hh].T


def _hgrn_prompt(qs, lf, kk, ib, gs, gout, s0t, lev, *, layer, batch, seq_len):
    m = qs.shape[0]
    rows = HGRN_STEP_CHUNKS * CHUNK
    assert seq_len % rows == 0
    nc = seq_len // rows
    row_spec = pl.BlockSpec((rows, SEG), lambda b, c: (b * nc + c, 0))
    state_spec = pl.BlockSpec((1, HEADS, HEAD_DIM, HEAD_DIM), lambda b, c: (b, 0, 0, 0))
    return pl.pallas_call(
        _hgrn_prompt_kernel,
        grid=(batch, nc),
        in_specs=[row_spec, row_spec, row_spec, row_spec, row_spec,
                  _layer_vec_spec(layer, SEG, 2),
                  state_spec,
                  pl.BlockSpec((CHUNK, CHUNK), lambda b, c: (0, 0))],
        out_specs=[row_spec, state_spec],
        out_shape=[jax.ShapeDtypeStruct((m, SEG), BF16),
                   jax.ShapeDtypeStruct((batch, HEADS, HEAD_DIM, HEAD_DIM), F32)],
        scratch_shapes=[pltpu.VMEM((HEADS, HEAD_DIM, HEAD_DIM), F32)],
        compiler_params=pltpu.CompilerParams(
            dimension_semantics=("arbitrary", "arbitrary"), vmem_limit_bytes=VMEM_LIMIT),
        name="hgrn_prompt",
    )(qs, lf, kk, ib, gs, gout, s0t, lev)


def _hgrn_decode_kernel(lft_ref, kkt_ref, qst_ref, ib_ref, gs_ref, gout_ref, s0_ref, *rest):
    ob_ref, snew_ref, o_scr = rest[-3:]
    tb = s0_ref.shape[0]
    for bb in range(tb):
        for hh in range(HEADS):
            ks = slice(hh * HEAD_DIM, (hh + 1) * HEAD_DIM)
            f_col = jnp.exp(lft_ref[0, ks, bb:bb + 1])
            k_col = kkt_ref[0, ks, bb:bb + 1].astype(F32)
            q_col = qst_ref[0, ks, bb:bb + 1].astype(F32)
            i_row = ib_ref[bb:bb + 1, ks].astype(F32)
            s_new = f_col * s0_ref[bb, hh] + k_col * i_row
            snew_ref[bb, hh] = s_new
            o_scr[bb:bb + 1, ks] = jnp.sum(q_col * s_new, axis=0, keepdims=True)
    for hh in range(HEADS):
        ks = slice(hh * HEAD_DIM, (hh + 1) * HEAD_DIM)
        o = o_scr[:, ks]
        on = o * lax.rsqrt(jnp.mean(o * o, axis=-1, keepdims=True) + EPS)
        ob_ref[:, ks] = (on * gout_ref[:, ks] * gs_ref[:, ks].astype(F32)).astype(BF16)


def _hgrn_decode(lft, kkt, qst, ib, gs, gout, s0_all, s_new_all, *, layer, tb):
    m = ib.shape[0]
    col_spec = pl.BlockSpec((1, SEG, tb), lambda i: (i, 0, 0))
    row_spec = pl.BlockSpec((tb, SEG), lambda i: (i, 0))
    state_spec = pl.BlockSpec((None, tb, HEADS, HEAD_DIM, HEAD_DIM), lambda i: (layer, i, 0, 0, 0))
    in_specs = [col_spec, col_spec, col_spec, row_spec, row_spec,
                _layer_vec_spec(layer, SEG, 1), state_spec]
    args = [lft, kkt, qst, ib, gs, gout, s0_all]
    aliases = {}
    if s_new_all is not None:
        in_specs.append(pl.BlockSpec(memory_space=pl.ANY))
        args.append(s_new_all)
        aliases = {len(args) - 1: 1}
    return pl.pallas_call(
        _hgrn_decode_kernel,
        grid=(m // tb,),
        in_specs=in_specs,
        out_specs=[row_spec, state_spec],
        out_shape=[jax.ShapeDtypeStruct((m, SEG), BF16),
                   jax.ShapeDtypeStruct(s0_all.shape, F32)],
        input_output_aliases=aliases,
        scratch_shapes=[pltpu.VMEM((tb, SEG), F32)],
        compiler_params=pltpu.CompilerParams(
            dimension_semantics=("arbitrary",), vmem_limit_bytes=VMEM_LIMIT),
        name="hgrn_decode",
    )(*args)


def _outproj_kernel(oa_ref, ob_ref, w_ref, x_ref, g_ref, o_ref):
    mix = _dot(oa_ref[...], w_ref[:SEG, :]) + _dot(ob_ref[...], w_ref[SEG:, :])
    o_ref[...] = x_ref[...] + _rms_norm(mix, g_ref[...])


def _outproj_cast_kernel(oa_ref, ob_ref, w_ref, x_ref, g_ref, o_ref, wb_ref, acc_scr):
    k = pl.program_id(0)
    wb = w_ref[...].astype(BF16)
    wb_ref[...] = wb

    @pl.when(k == 0)
    def _():
        acc_scr[...] = _dot(oa_ref[...], wb)

    @pl.when(k == 1)
    def _():
        o_ref[...] = x_ref[...] + _rms_norm(acc_scr[...] + _dot(ob_ref[...], wb), g_ref[...])


def _outproj_cast(oa, ob, w, x, g, *, layer):
    m, d = x.shape
    whole = lambda shape: pl.BlockSpec(shape, lambda k: (0,) * len(shape))
    return pl.pallas_call(
        _outproj_cast_kernel,
        grid=(2,),
        in_specs=[whole((m, SEG)), whole((m, SEG)),
                  pl.BlockSpec((None, SEG, d), lambda k: (layer, k, 0)),
                  whole((m, d)),
                  _layer_vec_spec(layer, d, 1)],
        out_specs=[whole((m, d)), pl.BlockSpec((SEG, d), lambda k: (k, 0))],
        out_shape=[jax.ShapeDtypeStruct((m, d), F32),
                   jax.ShapeDtypeStruct((2 * SEG, d), BF16)],
        scratch_shapes=[pltpu.VMEM((m, d), F32)],
        compiler_params=pltpu.CompilerParams(
            dimension_semantics=("arbitrary",), vmem_limit_bytes=VMEM_LIMIT),
        name="outproj_decode",
    )(oa, ob, w, x, g)


def _outproj(oa, ob, w, x, g, *, layer, tm):
    m, d = x.shape
    return pl.pallas_call(
        _outproj_kernel,
        grid=(m // tm,),
        in_specs=[pl.BlockSpec((tm, SEG), lambda i: (i, 0)),
                  pl.BlockSpec((tm, SEG), lambda i: (i, 0)),
                  pl.BlockSpec(w.shape, lambda i: (0, 0)),
                  pl.BlockSpec((tm, d), lambda i: (i, 0)),
                  _layer_vec_spec(layer, d, 1)],
        out_specs=pl.BlockSpec((tm, d), lambda i: (i, 0)),
        out_shape=jax.ShapeDtypeStruct((m, d), F32),
        compiler_params=pltpu.CompilerParams(
            dimension_semantics=("arbitrary",), vmem_limit_bytes=VMEM_LIMIT),
        name="outproj",
    )(oa, ob, w, x, g)


def _ffn_kernel(x_ref, gpre_ref, wg_ref, wu_ref, wd_ref, gpost_ref, o_ref, *rest, cast):
    h_scr = rest[-1]
    f = pl.program_id(1)
    nf = pl.num_programs(1)
    tm = x_ref.shape[0]
    if cast:
        for src, dst in zip((wg_ref, wu_ref, wd_ref), rest[:3]):
            dst[...] = src[...].astype(BF16)
        wg_ref, wu_ref, wd_ref = rest[:3]

    def contribution(rows):
        h = h_scr[rows, :]
        gate = _dot(h, wg_ref[...])
        up = _dot(h, wu_ref[...])
        t = (_silu(gate) * up).astype(BF16)
        return _dot(t, wd_ref[...])

    @pl.when(f == 0)
    def _():
        rc = min(tm, ROW_CHUNK)
        for r in range(tm // rc):
            rows = slice(r * rc, (r + 1) * rc)
            h_scr[rows, :] = _rms_norm(x_ref[rows, :], gpre_ref[...]).astype(BF16)
        o_ref[...] = contribution(slice(None))

    @pl.when(jnp.logical_and(f > 0, f < nf - 1))
    def _():
        o_ref[...] += contribution(slice(None))

    @pl.when(f == nf - 1)
    def _():
        rb = min(tm, FFN_LAST_ROW_BLOCK)
        for r in range(tm // rb):
            rows = slice(r * rb, (r + 1) * rb)
            acc = o_ref[rows, :] + contribution(rows)
            o_ref[rows, :] = x_ref[rows, :] + _rms_norm(acc, gpost_ref[...])


def _ffn(x, gpre, wg, wu, wd, gpost, *, layer, tm, tf, cast):
    m, d = x.shape
    dff = wg.shape[-1]
    assert dff // tf >= 2
    col_spec = pl.BlockSpec((d, tf), lambda i, f: (0, f))
    row_spec = pl.BlockSpec((tf, d), lambda i, f: (f, 0))
    out_specs = [pl.BlockSpec((tm, d), lambda i, f: (i, 0))]
    out_shape = [jax.ShapeDtypeStruct((m, d), F32)]
    if cast:
        assert m == tm
        w_specs = [pl.BlockSpec((None, d, tf), lambda i, f: (layer, 0, f)),
                   pl.BlockSpec((None, d, tf), lambda i, f: (layer, 0, f)),
                   pl.BlockSpec((None, tf, d), lambda i, f: (layer, f, 0))]
        out_specs += [col_spec, col_spec, row_spec]
        out_shape += [jax.ShapeDtypeStruct((d, dff), BF16), jax.ShapeDtypeStruct((d, dff), BF16),
                      jax.ShapeDtypeStruct((dff, d), BF16)]
    else:
        w_specs = [col_spec, col_spec, row_spec]
    res = pl.pallas_call(
        functools.partial(_ffn_kernel, cast=cast),
        grid=(m // tm, dff // tf),
        in_specs=[pl.BlockSpec((tm, d), lambda i, f: (i, 0)),
                  _layer_vec_spec(layer, d, 2),
                  *w_specs,
                  _layer_vec_spec(layer, d, 2)],
        out_specs=out_specs,
        out_shape=out_shape,
        scratch_shapes=[pltpu.VMEM((tm, d), BF16)],
        compiler_params=pltpu.CompilerParams(
            dimension_semantics=("arbitrary", "arbitrary"), vmem_limit_bytes=VMEM_LIMIT),
        name="ffn_decode" if cast else "ffn",
    )(x, gpre, wg, wu, wd, gpost)
    return res if cast else res[0]


PROMPT_TM = 1024
OUTPROJ_TM = 512
FFN_TF = 512
DECODE_TB = 16


def kernel(x_prompt, x_sample, state_hgrn, norm_mix_pre, norm_mix_post, w_in, ln_v_gain, ln_v_bias,
           spatial_w, spatial_b, lb_param, hgrn_out_gain, w_out, norm_ffn_pre, norm_ffn_post,
           w_gate, w_up, w_down):
    depth = w_in.shape[0]
    batch, seq_len, d = x_prompt.shape
    dec_batch, dec_seq, _ = x_sample.shape
    assert dec_seq == 1
    assert dec_batch % DECODE_TB == 0 and w_in.shape[2] == 6 * SEG and w_out.shape[1] == 2 * SEG

    lev = jnp.asarray(_level_table())
    vec = lambda a: a.reshape(depth, 1, -1)
    g_mix_pre, g_mix_post = vec(norm_mix_pre), vec(norm_mix_post)
    g_ffn_pre, g_ffn_post = vec(norm_ffn_pre), vec(norm_ffn_post)
    ln_g, ln_b, g_out = vec(ln_v_gain), vec(ln_v_bias), vec(hgrn_out_gain)
    spatial_bt = spatial_b.transpose(0, 2, 1)
    w00 = vec(jnp.repeat(spatial_w[:, :, 0, 0], HEAD_DIM, axis=1))
    b0 = vec(jnp.repeat(spatial_b[:, :, 0], HEAD_DIM, axis=1))

    xp = x_prompt.reshape(batch * seq_len, d)
    xs = x_sample.reshape(dec_batch, d)
    s0t_zero = jnp.zeros((batch, HEADS, HEAD_DIM, HEAD_DIM), F32)
    sp_out, vp_out, vs_out = [], [], []
    ss_all = None

    for l in range(depth):
        oa, vrows, hs, w_in_a = _inproj_a(
            xs, g_mix_pre, w_in, ln_g, ln_b, w00, b0, layer=l, decode=True, tm=dec_batch, seq_len=1)
        qs, lf, kk, ib, gs, w_in_b = _inproj_b(hs, w_in, lb_param, layer=l, decode=True, tm=dec_batch)
        cols = lambda a: a.reshape(dec_batch // DECODE_TB, DECODE_TB, SEG).transpose(0, 2, 1)
        ob, ss_all = _hgrn_decode(cols(lf), cols(kk), cols(qs), ib, gs, g_out, state_hgrn, ss_all,
                                  layer=l, tb=DECODE_TB)
        xs, w_out_b = _outproj_cast(oa, ob, w_out, xs, g_mix_post, layer=l)
        xs, w_gate_b, w_up_b, w_down_b = _ffn(xs, g_ffn_pre, w_gate, w_up, w_down, g_ffn_post,
                                              layer=l, tm=dec_batch, tf=FFN_TF, cast=True)
        vs_out.append(vrows.reshape(dec_batch, 1, SEG))
        oa, vrows, hp = _inproj_a(
            xp, g_mix_pre, w_in_a, ln_g, ln_b, spatial_w, spatial_bt,
            layer=l, decode=False, tm=PROMPT_TM, seq_len=seq_len)
        qs, lf, kk, ib, gs = _inproj_b(hp, w_in_b, lb_param, layer=l, decode=False, tm=PROMPT_TM)
        ob, s_fin = _hgrn_prompt(qs, lf, kk, ib, gs, g_out, s0t_zero, lev,
                                 layer=l, batch=batch, seq_len=seq_len)
        xp = _outproj(oa, ob, w_out_b, xp, g_mix_post, layer=l, tm=OUTPROJ_TM)
        xp = _ffn(xp, g_ffn_pre, w_gate_b, w_up_b, w_down_b, g_ffn_post,
                  layer=l, tm=PROMPT_TM, tf=FFN_TF, cast=False)
        sp_out.append(s_fin)
        vp_out.append(vrows.reshape(batch, CHUNK, SEG))

    return (xp.reshape(batch, seq_len, d), xs.reshape(dec_batch, 1, d),
            jnp.stack(sp_out), ss_all, jnp.stack(vp_out), jnp.stack(vs_out))
```

```python
import functools

import numpy as np
import jax
import jax.numpy as jnp
from jax import lax
from jax.experimental import pallas as pl
from jax.experimental.pallas import tpu as pltpu

F32 = jnp.float32
BF16 = jnp.bfloat16

EPS = 1e-6
LANES = 128
SUBLANES = 8
SEG = 1024
HEADS = 8
HEAD_DIM = SEG // HEADS
CHUNK = 128
N_LEVELS = 7
HGRN_STEP_CHUNKS = 8
ROW_CHUNK = 256
COL_CHUNK = 256
V_ROW_BLOCK = 1024
FFN_LAST_ROW_BLOCK = 512
LOG2E = np.float32(1.4426950408889634)
VMEM_LIMIT = 60 * 1024 * 1024


def _rms_norm(x, g):
    return x * lax.rsqrt(jnp.mean(x * x, axis=-1, keepdims=True) + EPS) * g


def _silu(x):
    return x * (0.5 + 0.5 * jnp.tanh(0.5 * x))


def _gelu(x):
    return 0.5 * x * (1.0 + lax.erf(x * np.float32(np.sqrt(0.5))))


def _dot(a, b):
    return jnp.dot(a, b, preferred_element_type=F32)


def _dot_nt(a, b):
    return lax.dot_general(a, b, (((1,), (1,)), ((), ())), preferred_element_type=F32)


def _dot_tn(a, b):
    return lax.dot_general(a, b, (((0,), (0,)), ((), ())), preferred_element_type=F32)


def _forget_lower_bound(lbp, layer):
    m = jnp.max(lbp, axis=0, keepdims=True)
    e = jnp.exp(lbp - m)
    sm = e / jnp.sum(e, axis=0, keepdims=True)
    cum = sm[0:1]
    for r in range(1, layer + 1):
        cum = cum + sm[r:r + 1]
    return cum - sm[0:1]


def _layer_vec_spec(layer, width, grid_rank):
    if grid_rank == 1:
        return pl.BlockSpec((None, 1, width), lambda i: (layer, 0, 0))
    return pl.BlockSpec((None, 1, width), lambda i, j: (layer, 0, 0))


def _by_columns(h_ref, w_ref, epilogue):
    for c in range(SEG // COL_CHUNK):
        cols = slice(c * COL_CHUNK, (c + 1) * COL_CHUNK)
        epilogue(_dot(h_ref[...], w_ref[:, cols]), cols)


def _inproj_a_kernel(x_ref, gpre_ref, w_ref, lng_ref, lnb_ref, ws_ref, bs_ref,
                     oa_ref, vrows_ref, h_ref, *rest, decode):
    wb_ref = rest[0] if decode else None
    u_scr = rest[-1]
    j = pl.program_id(1)
    tm = x_ref.shape[0]
    if decode:
        wb_ref[...] = w_ref[...].astype(BF16)
    wmat_ref = wb_ref if decode else w_ref

    @pl.when(j == 0)
    def _():
        rc = min(tm, ROW_CHUNK)
        for r in range(tm // rc):
            rows = slice(r * rc, (r + 1) * rc)
            h_ref[rows, :] = _rms_norm(x_ref[rows, :], gpre_ref[...]).astype(BF16)

        def epilogue(acc, cols):
            u_scr[:, cols] = _gelu(acc)
        _by_columns(h_ref, wmat_ref, epilogue)

    @pl.when(j == 1)
    def _():
        if not decode:
            row = lax.broadcasted_iota(jnp.int32, (CHUNK, CHUNK), 0)
            col = lax.broadcasted_iota(jnp.int32, (CHUNK, CHUNK), 1)
            causal = row >= col
        rb = min(tm, V_ROW_BLOCK)
        for r in range(tm // rb):
            rows = slice(r * rb, (r + 1) * rb)
            gv = _gelu(_dot(h_ref[rows, :], wmat_ref[...]))
            mu = jnp.mean(gv, axis=-1, keepdims=True)
            vc = gv - mu
            v = vc * lax.rsqrt(jnp.mean(vc * vc, axis=-1, keepdims=True) + EPS)
            v = v * lng_ref[...] + lnb_ref[...]
            if decode:
                vrows_ref[rows, :] = v
                oa_ref[rows, :] = (u_scr[rows, :] * (v * ws_ref[...] + bs_ref[...])).astype(BF16)
                continue
            if r == tm // rb - 1:
                vrows_ref[...] = v[rb - CHUNK:, :]
            vb = v.astype(BF16)
            for hh in range(HEADS):
                cs = slice(hh * HEAD_DIM, (hh + 1) * HEAD_DIM)
                w_h = jnp.where(causal, ws_ref[hh], 0.0).astype(BF16)
                b_h = bs_ref[:, hh:hh + 1]
                for c in range(rb // CHUNK):
                    s = _dot(w_h, vb[c * CHUNK:(c + 1) * CHUNK, cs]) + b_h
                    rs = slice(rows.start + c * CHUNK, rows.start + (c + 1) * CHUNK)
                    oa_ref[rs, cs] = (u_scr[rs, cs] * s).astype(BF16)


def _inproj_a(x, gpre, w, lng, lnb, ws, bs, *, layer, decode, tm, seq_len):
    m, d = x.shape
    if decode:
        assert m == tm
        vrows_shape, vrows_spec = (m, SEG), pl.BlockSpec((tm, SEG), lambda i, j: (i, 0))
        ws_spec = _layer_vec_spec(layer, SEG, 2)
        bs_spec = _layer_vec_spec(layer, SEG, 2)
        w_spec = pl.BlockSpec((None, d, SEG), lambda i, j: (layer, 0, j))
    else:
        assert seq_len % tm == 0 and tm % CHUNK == 0
        tiles_per_seq = seq_len // tm
        vrows_shape = (m // seq_len * CHUNK, SEG)
        vrows_spec = pl.BlockSpec((CHUNK, SEG), lambda i, j: (i // tiles_per_seq, 0))
        ws_spec = pl.BlockSpec((None, HEADS, CHUNK, CHUNK), lambda i, j: (layer, 0, 0, 0))
        bs_spec = pl.BlockSpec((None, CHUNK, HEADS), lambda i, j: (layer, 0, 0))
        w_spec = pl.BlockSpec((d, SEG), lambda i, j: (0, j))
    out_shape = [jax.ShapeDtypeStruct((m, SEG), BF16),
                 jax.ShapeDtypeStruct(vrows_shape, F32),
                 jax.ShapeDtypeStruct((m, d), BF16)]
    out_specs = [pl.BlockSpec((tm, SEG), lambda i, j: (i, 0)), vrows_spec,
                 pl.BlockSpec((tm, d), lambda i, j: (i, 0))]
    if decode:
        out_shape.append(jax.ShapeDtypeStruct((d, 2 * SEG), BF16))
        out_specs.append(pl.BlockSpec((d, SEG), lambda i, j: (0, j)))
    return pl.pallas_call(
        functools.partial(_inproj_a_kernel, decode=decode),
        grid=(m // tm, 2),
        in_specs=[pl.BlockSpec((tm, d), lambda i, j: (i, 0)),
                  _layer_vec_spec(layer, d, 2),
                  w_spec,
                  _layer_vec_spec(layer, SEG, 2), _layer_vec_spec(layer, SEG, 2), ws_spec, bs_spec],
        out_specs=out_specs,
        out_shape=out_shape,
        scratch_shapes=[pltpu.VMEM((tm, SEG), F32)],
        compiler_params=pltpu.CompilerParams(
            dimension_semantics=("arbitrary", "arbitrary"), vmem_limit_bytes=VMEM_LIMIT),
        name="inproj_a_decode" if decode else "inproj_a",
    )(x, gpre, w, lng, lnb, ws, bs)


def _inproj_b_kernel(h_ref, w_ref, lbp_ref, qs_ref, lf_ref, kk_ref, ib_ref, gs_ref, *rest,
                     layer, decode):
    j = pl.program_id(1)
    if decode:
        wb_ref = rest[0]
        wb_ref[...] = w_ref[...].astype(BF16)
        w_ref = wb_ref

    @pl.when(j == 0)
    def _():
        def epilogue(acc, cols):
            qs_ref[:, cols] = _silu(acc).astype(BF16)
        _by_columns(h_ref, w_ref, epilogue)

    @pl.when(j == 1)
    def _():
        lb_all = _forget_lower_bound(lbp_ref[...], layer)

        def epilogue(z, cols):
            ez = jnp.exp(-jnp.abs(z))
            one_plus = 1.0 + ez
            inv = 1.0 / one_plus
            sig_neg = jnp.where(z >= 0.0, ez, 1.0) * inv
            if layer == 0:
                lf_ref[:, cols] = jnp.minimum(z, 0.0) - jnp.log(one_plus)
                kk_ref[:, cols] = sig_neg.astype(BF16)
            else:
                lb = lb_all[:, cols]
                f = lb + (1.0 - lb) * (jnp.where(z >= 0.0, 1.0, ez) * inv)
                lf_ref[:, cols] = jnp.where(f > 0.0, jnp.log(f), jnp.minimum(z, 0.0))
                kk_ref[:, cols] = ((1.0 - lb) * sig_neg).astype(BF16)
        _by_columns(h_ref, w_ref, epilogue)

    @pl.when(j == 2)
    def _():
        def epilogue(acc, cols):
            ib_ref[:, cols] = acc.astype(BF16)
        _by_columns(h_ref, w_ref, epilogue)

    @pl.when(j == 3)
    def _():
        def epilogue(acc, cols):
            gs_ref[:, cols] = _silu(acc).astype(BF16)
        _by_columns(h_ref, w_ref, epilogue)


def _inproj_b(h, w, lbp, *, layer, decode, tm):
    m, d = h.shape
    n_seg = 4
    if decode:
        assert m == tm
        w_spec = pl.BlockSpec((None, d, SEG), lambda i, j: (layer, 0, j + 2))
    else:
        w_spec = pl.BlockSpec((d, SEG), lambda i, j: (0, j))
    row_spec = pl.BlockSpec((tm, SEG), lambda i, j: (i, 0))
    out_shape = [jax.ShapeDtypeStruct((m, SEG), BF16),
                 jax.ShapeDtypeStruct((m, SEG), F32),
                 jax.ShapeDtypeStruct((m, SEG), BF16),
                 jax.ShapeDtypeStruct((m, SEG), BF16),
                 jax.ShapeDtypeStruct((m, SEG), BF16)]
    out_specs = [row_spec] * 5
    if decode:
        out_shape.append(jax.ShapeDtypeStruct((d, n_seg * SEG), BF16))
        out_specs.append(pl.BlockSpec((d, SEG), lambda i, j: (0, j)))
    return pl.pallas_call(
        functools.partial(_inproj_b_kernel, layer=layer, decode=decode),
        grid=(m // tm, n_seg),
        in_specs=[pl.BlockSpec((tm, d), lambda i, j: (i, 0)),
                  w_spec,
                  pl.BlockSpec(lbp.shape, lambda i, j: (0, 0))],
        out_specs=out_specs,
        out_shape=out_shape,
        compiler_params=pltpu.CompilerParams(
            dimension_semantics=("arbitrary", "arbitrary"), vmem_limit_bytes=VMEM_LIMIT),
        name="inproj_b_decode" if decode else "inproj_b",
    )(h, w, lbp)


def _level_table():
    t = np.arange(CHUNK)[:, None]
    s = np.arange(CHUNK)[None, :]
    x = t ^ s
    lev = np.floor(np.log2(np.maximum(x, 1))).astype(np.int32)
    lev = np.where(t == s, -1, lev)
    lev = np.where(t < s, -2, lev)
    return lev.astype(np.int32)


def _log_gap_to_block_boundary(b, level):
    half = 1 << level
    blk = 2 * half
    k = b.shape[1]
    if half >= SUBLANES:
        bb = b.reshape(CHUNK // blk, blk, k)
        r = bb[:, half - 1:half, :]
        gap = jnp.concatenate([r - bb[:, :half, :], bb[:, half:, :] - r], axis=1)
        return gap.reshape(CHUNK, k)
    b3 = b.reshape(CHUNK // SUBLANES, SUBLANES, k)
    sub = lax.broadcasted_iota(jnp.int32, (1, SUBLANES, k), 1)
    r = None
    for start in range(0, SUBLANES, blk):
        cand = jnp.broadcast_to(b3[:, start + half - 1:start + half, :], b3.shape)
        r = cand if r is None else jnp.where(sub >= start, cand, r)
    sign = jnp.where((sub & half) != 0, 1.0, -1.0)
    return ((b3 - r) * sign).reshape(CHUNK, k)


def _second_half_rows_from(a, b, level):
    half = 1 << level
    blk = 2 * half
    k = a.shape[1]
    if half >= SUBLANES:
        a3 = a.reshape(CHUNK // blk, blk, k)
        b3 = b.reshape(CHUNK // blk, blk, k)
        return jnp.concatenate([b3[:, :half, :], a3[:, half:, :]], axis=1).reshape(CHUNK, k)
    sub = lax.broadcasted_iota(jnp.int32, (1, SUBLANES, k), 1)
    a3 = a.reshape(CHUNK // SUBLANES, SUBLANES, k)
    b3 = b.reshape(CHUNK // SUBLANES, SUBLANES, k)
    return jnp.where((sub & half) != 0, a3, b3).reshape(CHUNK, k)


def _hgrn_chunk(qs_ref, lf_ref, kk_ref, ib_ref, gs_ref, gout_ref, ob_ref, st_scr, lev, tri):
    odd_row = (lax.broadcasted_iota(jnp.int32, (CHUNK, HEAD_DIM), 0) & 1) == 1
    heads = [slice(hh * HEAD_DIM, (hh + 1) * HEAD_DIM) for hh in range(HEADS)]
    lf2, q, kk, b, o, a = [], [], [], [], [], []
    for hh, cs in enumerate(heads):
        lf2.append(lf_ref[:, cs] * LOG2E)
        hi = lf2[hh].astype(BF16)
        r1 = lf2[hh] - hi.astype(F32)
        mid = r1.astype(BF16)
        lo = (r1 - mid.astype(F32)).astype(BF16)
        b.append(_dot(tri, hi) + _dot(tri, mid) + _dot(tri, lo))
    for hh, cs in enumerate(heads):
        qb = qs_ref[:, cs]
        kb = kk_ref[:, cs]
        q.append(qb.astype(F32))
        kk.append(kb.astype(F32))
        o.append(_dot_nt((q[hh] * jnp.exp2(b[hh])).astype(BF16), st_scr[hh].astype(BF16)))
        a.append(jnp.where(lev == -1, _dot_nt(qb, kb), 0.0))

    for level in range(N_LEVELS):
        for hh in range(HEADS):
            if level == 0:
                d = jnp.where(odd_row, lf2[hh], 0.0)
            else:
                d = _log_gap_to_block_boundary(b[hh], level)
            x = (_second_half_rows_from(q[hh], kk[hh], level) * jnp.exp2(d)).astype(BF16)
            a[hh] = jnp.where(lev == level, _dot_nt(x, x), a[hh])

    for hh, cs in enumerate(heads):
        o[hh] = o[hh] + _dot(a[hh].astype(BF16), ib_ref[:, cs])
    for hh, cs in enumerate(heads):
        b_last = b[hh][CHUNK - 1:CHUNK, :]
        kd = (kk[hh] * jnp.exp2(b_last - b[hh])).astype(BF16)
        st_scr[hh] = st_scr[hh] * jnp.exp2(b_last) + _dot_tn(ib_ref[:, cs], kd)
    for hh, cs in enumerate(heads):
        on = o[hh] * lax.rsqrt(jnp.mean(o[hh] * o[hh], axis=-1, keepdims=True) + EPS)
        ob_ref[:, cs] = (on * gout_ref[:, cs] * gs_ref[:, cs].astype(F32)).astype(BF16)


def _hgrn_prompt_kernel(qs_ref, lf_ref, kk_ref, ib_ref, gs_ref, gout_ref, s0t_ref, lev_ref,
                        ob_ref, sfin_ref, st_scr):
    c = pl.program_id(1)

    @pl.when(c == 0)
    def _():
        st_scr[...] = s0t_ref[0]

    lev = lev_ref[...]
    row = lax.broadcasted_iota(jnp.int32, (CHUNK, CHUNK), 0)
    col = lax.broadcasted_iota(jnp.int32, (CHUNK, CHUNK), 1)
    tri = jnp.where(row >= col, 1.0, 0.0).astype(BF16)

    for ci in range(lf_ref.shape[0] // CHUNK):
        rows = pl.ds(ci * CHUNK, CHUNK)
        _hgrn_chunk(qs_ref.at[rows], lf_ref.at[rows], kk_ref.at[rows], ib_ref.at[rows],
                    gs_ref.at[rows], gout_ref, ob_ref.at[rows], st_scr, lev, tri)

    @pl.when(c == pl.num_programs(1) - 1)
    def _():
        for hh in range(HEADS):
            sfin_ref[0, hh] = st_scr[hh].T


def _hgrn_prompt(qs, lf, kk, ib, gs, gout, s0t, lev, *, layer, batch, seq_len):
    m = qs.shape[0]
    rows = HGRN_STEP_CHUNKS * CHUNK
    assert seq_len % rows == 0
    nc = seq_len // rows
    row_spec = pl.BlockSpec((rows, SEG), lambda b, c: (b * nc + c, 0))
    state_spec = pl.BlockSpec((1, HEADS, HEAD_DIM, HEAD_DIM), lambda b, c: (b, 0, 0, 0))
    return pl.pallas_call(
        _hgrn_prompt_kernel,
        grid=(batch, nc),
        in_specs=[row_spec, row_spec, row_spec, row_spec, row_spec,
                  _layer_vec_spec(layer, SEG, 2),
                  state_spec,
                  pl.BlockSpec((CHUNK, CHUNK), lambda b, c: (0, 0))],
        out_specs=[row_spec, state_spec],
        out_shape=[jax.ShapeDtypeStruct((m, SEG), BF16),
                   jax.ShapeDtypeStruct((batch, HEADS, HEAD_DIM, HEAD_DIM), F32)],
        scratch_shapes=[pltpu.VMEM((HEADS, HEAD_DIM, HEAD_DIM), F32)],
        compiler_params=pltpu.CompilerParams(
            dimension_semantics=("arbitrary", "arbitrary"), vmem_limit_bytes=VMEM_LIMIT),
        name="hgrn_prompt",
    )(qs, lf, kk, ib, gs, gout, s0t, lev)


def _hgrn_decode_kernel(lft_ref, kkt_ref, qst_ref, ib_ref, gs_ref, gout_ref, s0_ref, *rest):
    ob_ref, snew_ref, o_scr = rest[-3:]
    tb = s0_ref.shape[0]
    for bb in range(tb):
        for hh in range(HEADS):
            ks = slice(hh * HEAD_DIM, (hh + 1) * HEAD_DIM)
            f_col = jnp.exp(lft_ref[0, ks, bb:bb + 1])
            k_col = kkt_ref[0, ks, bb:bb + 1].astype(F32)
            q_col = qst_ref[0, ks, bb:bb + 1].astype(F32)
            i_row = ib_ref[bb:bb + 1, ks].astype(F32)
            s_new = f_col * s0_ref[bb, hh] + k_col * i_row
            snew_ref[bb, hh] = s_new
            o_scr[bb:bb + 1, ks] = jnp.sum(q_col * s_new, axis=0, keepdims=True)
    for hh in range(HEADS):
        ks = slice(hh * HEAD_DIM, (hh + 1) * HEAD_DIM)
        o = o_scr[:, ks]
        on = o * lax.rsqrt(jnp.mean(o * o, axis=-1, keepdims=True) + EPS)
        ob_ref[:, ks] = (on * gout_ref[:, ks] * gs_ref[:, ks].astype(F32)).astype(BF16)


def _hgrn_decode(lft, kkt, qst, ib, gs, gout, s0_all, s_new_all, *, layer, tb):
    m = ib.shape[0]
    col_spec = pl.BlockSpec((1, SEG, tb), lambda i: (i, 0, 0))
    row_spec = pl.BlockSpec((tb, SEG), lambda i: (i, 0))
    state_spec = pl.BlockSpec((None, tb, HEADS, HEAD_DIM, HEAD_DIM), lambda i: (layer, i, 0, 0, 0))
    in_specs = [col_spec, col_spec, col_spec, row_spec, row_spec,
                _layer_vec_spec(layer, SEG, 1), state_spec]
    args = [lft, kkt, qst, ib, gs, gout, s0_all]
    aliases = {}
    if s_new_all is not None:
        in_specs.append(pl.BlockSpec(memory_space=pl.ANY))
        args.append(s_new_all)
        aliases = {len(args) - 1: 1}
    return pl.pallas_call(
        _hgrn_decode_kernel,
        grid=(m // tb,),
        in_specs=in_specs,
        out_specs=[row_spec, state_spec],
        out_shape=[jax.ShapeDtypeStruct((m, SEG), BF16),
                   jax.ShapeDtypeStruct(s0_all.shape, F32)],
        input_output_aliases=aliases,
        scratch_shapes=[pltpu.VMEM((tb, SEG), F32)],
        compiler_params=pltpu.CompilerParams(
            dimension_semantics=("arbitrary",), vmem_limit_bytes=VMEM_LIMIT),
        name="hgrn_decode",
    )(*args)


def _outproj_kernel(oa_ref, ob_ref, w_ref, x_ref, g_ref, o_ref):
    mix = _dot(oa_ref[...], w_ref[:SEG, :]) + _dot(ob_ref[...], w_ref[SEG:, :])
    o_ref[...] = x_ref[...] + _rms_norm(mix, g_ref[...])


def _outproj_cast_kernel(oa_ref, ob_ref, w_ref, x_ref, g_ref, o_ref, wb_ref, acc_scr):
    k = pl.program_id(0)
    wb = w_ref[...].astype(BF16)
    wb_ref[...] = wb

    @pl.when(k == 0)
    def _():
        acc_scr[...] = _dot(oa_ref[...], wb)

    @pl.when(k == 1)
    def _():
        o_ref[...] = x_ref[...] + _rms_norm(acc_scr[...] + _dot(ob_ref[...], wb), g_ref[...])


def _outproj_cast(oa, ob, w, x, g, *, layer):
    m, d = x.shape
    whole = lambda shape: pl.BlockSpec(shape, lambda k: (0,) * len(shape))
    return pl.pallas_call(
        _outproj_cast_kernel,
        grid=(2,),
        in_specs=[whole((m, SEG)), whole((m, SEG)),
                  pl.BlockSpec((None, SEG, d), lambda k: (layer, k, 0)),
                  whole((m, d)),
                  _layer_vec_spec(layer, d, 1)],
        out_specs=[whole((m, d)), pl.BlockSpec((SEG, d), lambda k: (k, 0))],
        out_shape=[jax.ShapeDtypeStruct((m, d), F32),
                   jax.ShapeDtypeStruct((2 * SEG, d), BF16)],
        scratch_shapes=[pltpu.VMEM((m, d), F32)],
        compiler_params=pltpu.CompilerParams(
            dimension_semantics=("arbitrary",), vmem_limit_bytes=VMEM_LIMIT),
        name="outproj_decode",
    )(oa, ob, w, x, g)


def _outproj(oa, ob, w, x, g, *, layer, tm):
    m, d = x.shape
    return pl.pallas_call(
        _outproj_kernel,
        grid=(m // tm,),
        in_specs=[pl.BlockSpec((tm, SEG), lambda i: (i, 0)),
                  pl.BlockSpec((tm, SEG), lambda i: (i, 0)),
                  pl.BlockSpec(w.shape, lambda i: (0, 0)),
                  pl.BlockSpec((tm, d), lambda i: (i, 0)),
                  _layer_vec_spec(layer, d, 1)],
        out_specs=pl.BlockSpec((tm, d), lambda i: (i, 0)),
        out_shape=jax.ShapeDtypeStruct((m, d), F32),
        compiler_params=pltpu.CompilerParams(
            dimension_semantics=("arbitrary",), vmem_limit_bytes=VMEM_LIMIT),
        name="outproj",
    )(oa, ob, w, x, g)


def _ffn_kernel(x_ref, gpre_ref, wg_ref, wu_ref, wd_ref, gpost_ref, o_ref, *rest, cast):
    h_scr = rest[-1]
    f = pl.program_id(1)
    nf = pl.num_programs(1)
    tm = x_ref.shape[0]
    if cast:
        for src, dst in zip((wg_ref, wu_ref, wd_ref), rest[:3]):
            dst[...] = src[...].astype(BF16)
        wg_ref, wu_ref, wd_ref = rest[:3]

    def contribution(rows):
        h = h_scr[rows, :]
        gate = _dot(h, wg_ref[...])
        up = _dot(h, wu_ref[...])
        t = (_silu(gate) * up).astype(BF16)
        return _dot(t, wd_ref[...])

    @pl.when(f == 0)
    def _():
        rc = min(tm, ROW_CHUNK)
        for r in range(tm // rc):
            rows = slice(r * rc, (r + 1) * rc)
            h_scr[rows, :] = _rms_norm(x_ref[rows, :], gpre_ref[...]).astype(BF16)
        o_ref[...] = contribution(slice(None))

    @pl.when(jnp.logical_and(f > 0, f < nf - 1))
    def _():
        o_ref[...] += contribution(slice(None))

    @pl.when(f == nf - 1)
    def _():
        rb = min(tm, FFN_LAST_ROW_BLOCK)
        for r in range(tm // rb):
            rows = slice(r * rb, (r + 1) * rb)
            acc = o_ref[rows, :] + contribution(rows)
            o_ref[rows, :] = x_ref[rows, :] + _rms_norm(acc, gpost_ref[...])


def _ffn(x, gpre, wg, wu, wd, gpost, *, layer, tm, tf, cast):
    m, d = x.shape
    dff = wg.shape[-1]
    assert dff // tf >= 2
    col_spec = pl.BlockSpec((d, tf), lambda i, f: (0, f))
    row_spec = pl.BlockSpec((tf, d), lambda i, f: (f, 0))
    out_specs = [pl.BlockSpec((tm, d), lambda i, f: (i, 0))]
    out_shape = [jax.ShapeDtypeStruct((m, d), F32)]
    if cast:
        assert m == tm
        w_specs = [pl.BlockSpec((None, d, tf), lambda i, f: (layer, 0, f)),
                   pl.BlockSpec((None, d, tf), lambda i, f: (layer, 0, f)),
                   pl.BlockSpec((None, tf, d), lambda i, f: (layer, f, 0))]
        out_specs += [col_spec, col_spec, row_spec]
        out_shape += [jax.ShapeDtypeStruct((d, dff), BF16), jax.ShapeDtypeStruct((d, dff), BF16),
                      jax.ShapeDtypeStruct((dff, d), BF16)]
    else:
        w_specs = [col_spec, col_spec, row_spec]
    res = pl.pallas_call(
        functools.partial(_ffn_kernel, cast=cast),
        grid=(m // tm, dff // tf),
        in_specs=[pl.BlockSpec((tm, d), lambda i, f: (i, 0)),
                  _layer_vec_spec(layer, d, 2),
                  *w_specs,
                  _layer_vec_spec(layer, d, 2)],
        out_specs=out_specs,
        out_shape=out_shape,
        scratch_shapes=[pltpu.VMEM((tm, d), BF16)],
        compiler_params=pltpu.CompilerParams(
            dimension_semantics=("arbitrary", "arbitrary"), vmem_limit_bytes=VMEM_LIMIT),
        name="ffn_decode" if cast else "ffn",
    )(x, gpre, wg, wu, wd, gpost)
    return res if cast else res[0]


PROMPT_TM = 1024
OUTPROJ_TM = 512
FFN_TF = 512
DECODE_TB = 16


def kernel(x_prompt, x_sample, state_hgrn, norm_mix_pre, norm_mix_post, w_in, ln_v_gain, ln_v_bias,
           spatial_w, spatial_b, lb_param, hgrn_out_gain, w_out, norm_ffn_pre, norm_ffn_post,
           w_gate, w_up, w_down):
    depth = w_in.shape[0]
    batch, seq_len, d = x_prompt.shape
    dec_batch, dec_seq, _ = x_sample.shape
    assert dec_seq == 1
    assert dec_batch % DECODE_TB == 0 and w_in.shape[2] == 6 * SEG and w_out.shape[1] == 2 * SEG

    lev = jnp.asarray(_level_table())
    vec = lambda a: a.reshape(depth, 1, -1)
    g_mix_pre, g_mix_post = vec(norm_mix_pre), vec(norm_mix_post)
    g_ffn_pre, g_ffn_post = vec(norm_ffn_pre), vec(norm_ffn_post)
    ln_g, ln_b, g_out = vec(ln_v_gain), vec(ln_v_bias), vec(hgrn_out_gain)
    spatial_bt = spatial_b.transpose(0, 2, 1)
    w00 = vec(jnp.repeat(spatial_w[:, :, 0, 0], HEAD_DIM, axis=1))
    b0 = vec(jnp.repeat(spatial_b[:, :, 0], HEAD_DIM, axis=1))

    xp = x_prompt.reshape(batch * seq_len, d)
    xs = x_sample.reshape(dec_batch, d)
    s0t_zero = jnp.zeros((batch, HEADS, HEAD_DIM, HEAD_DIM), F32)
    sp_out, vp_out, vs_out = [], [], []
    ss_all = None

    for l in range(depth):
        oa, vrows, hs, w_in_a = _inproj_a(
            xs, g_mix_pre, w_in, ln_g, ln_b, w00, b0, layer=l, decode=True, tm=dec_batch, seq_len=1)
        qs, lf, kk, ib, gs, w_in_b = _inproj_b(hs, w_in, lb_param, layer=l, decode=True, tm=dec_batch)
        cols = lambda a: a.reshape(dec_batch // DECODE_TB, DECODE_TB, SEG).transpose(0, 2, 1)
        ob, ss_all = _hgrn_decode(cols(lf), cols(kk), cols(qs), ib, gs, g_out, state_hgrn, ss_all,
                                  layer=l, tb=DECODE_TB)
        xs, w_out_b = _outproj_cast(oa, ob, w_out, xs, g_mix_post, layer=l)
        xs, w_gate_b, w_up_b, w_down_b = _ffn(xs, g_ffn_pre, w_gate, w_up, w_down, g_ffn_post,
                                              layer=l, tm=dec_batch, tf=FFN_TF, cast=True)
        vs_out.append(vrows.reshape(dec_batch, 1, SEG))
        oa, vrows, hp = _inproj_a(
            xp, g_mix_pre, w_in_a, ln_g, ln_b, spatial_w, spatial_bt,
            layer=l, decode=False, tm=PROMPT_TM, seq_len=seq_len)
        qs, lf, kk, ib, gs = _inproj_b(hp, w_in_b, lb_param, layer=l, decode=False, tm=PROMPT_TM)
        ob, s_fin = _hgrn_prompt(qs, lf, kk, ib, gs, g_out, s0t_zero, lev,
                                 layer=l, batch=batch, seq_len=seq_len)
        xp = _outproj(oa, ob, w_out_b, xp, g_mix_post, layer=l, tm=OUTPROJ_TM)
        xp = _ffn(xp, g_ffn_pre, w_gate_b, w_up_b, w_down_b, g_ffn_post,
                  layer=l, tm=PROMPT_TM, tf=FFN_TF, cast=False)
        sp_out.append(s_fin)
        vp_out.append(vrows.reshape(batch, CHUNK, SEG))

    return (xp.reshape(batch, seq_len, d), xs.reshape(dec_batch, 1, d),
            jnp.stack(sp_out), ss_all, jnp.stack(vp_out), jnp.stack(vs_out))
```
